```python
import math
import jax, jax.numpy as jnp
from jax import lax
import numpy as np

D_MODEL = 1024
BATCH = 4
SEQ = 8192
DEPTH = 4

MEM_LEN = 256
HEAD_DIM = 64
MIX_WIDTH = 3 * D_MODEL // 4
MEM_WIDTH = D_MODEL - MIX_WIDTH
N_SB_HEADS = MIX_WIDTH // HEAD_DIM
DIFF_V_DIM = 2 * HEAD_DIM
N_DIFF_HEADS = MIX_WIDTH // DIFF_V_DIM
N_MEM_HEADS = 4
MEM_HEAD_DIM = MEM_WIDTH // N_MEM_HEADS
IN_WIDTH = 3 * MIX_WIDTH + MEM_WIDTH
ROPE_DIM = HEAD_DIM // 4
ROPE_THETA = 500000.0
BLOCK_Q = 128
D_FF_DENSE = ((8 * D_MODEL // 3 + 255) // 256) * 256
N_EXPERTS = 8
TOP_K = 2
D_FF_EXPERT = 7 * D_MODEL // 2
N_EVEN = (DEPTH + 1) // 2
N_ODD = DEPTH // 2
EPS = 1e-6
NEG_BIG = -1e30
POS_OFFSET_MAX = 4096

kernel_name = "hybrid_sb_diff_mem_moe_trunk"


def rms_norm(x, g):
    xf = x.astype(jnp.float32)
    y = xf * lax.rsqrt(jnp.mean(xf * xf, axis=-1, keepdims=True) + EPS)
    return (y * g.astype(jnp.float32)).astype(x.dtype)


def to_heads(t, n_heads, dim):
    b, s, _ = t.shape
    return t.reshape(b, s, n_heads, dim).transpose(0, 2, 1, 3)


def merge_heads(t):
    b, h, s, d = t.shape
    return t.transpose(0, 2, 1, 3).reshape(b, s, h * d)


def rope_tables(positions):
    inv_freq = ROPE_THETA ** (-jnp.arange(0, ROPE_DIM, 2, dtype=jnp.float32) / ROPE_DIM)
    ang = positions.astype(jnp.float32)[..., None] * inv_freq
    return jnp.cos(ang)[:, None], jnp.sin(ang)[:, None]


def partial_rope(t, cos, sin):
    half = ROPE_DIM // 2
    t1, t2, rest = t[..., :half], t[..., half:ROPE_DIM], t[..., ROPE_DIM:]
    c, s = cos.astype(t.dtype), sin.astype(t.dtype)
    return jnp.concatenate([t1 * c - t2 * s, t2 * c + t1 * s, rest], axis=-1)


def stick_breaking_attention(q, k, v):
    b, h, s, d = q.shape
    nb = s // BLOCK_Q
    scale = d ** -0.5
    qb = q.reshape(b, h, nb, BLOCK_Q, d).transpose(2, 0, 1, 3, 4)
    key_idx = jnp.arange(s)

    def block(args):
        qi, i = args
        z = jnp.einsum('bhqd,bhkd->bhqk', qi, k, preferred_element_type=jnp.float32) * scale
        q_idx = i * BLOCK_Q + jnp.arange(BLOCK_Q)
        strict = key_idx[None, :] < q_idx[:, None]
        log_beta = jax.nn.log_sigmoid(z)
        log_keep = jnp.where(strict, jax.nn.log_sigmoid(-z), 0.0)
        after = lax.cumsum(log_keep, axis=3, reverse=True) - log_keep
        w = jnp.where(strict, jnp.exp(log_beta + after), 0.0)
        return jnp.einsum('bhqk,bhkd->bhqd', w.astype(v.dtype), v)

    out = lax.map(block, (qb, jnp.arange(nb)))
    return out.transpose(1, 2, 0, 3, 4).reshape(b, h, s, d)


def differential_attention(q, k, v, lam):
    b, h, _, s, d = q.shape
    nb = s // BLOCK_Q
    scale = d ** -0.5
    qb = q.reshape(b, h, 2, nb, BLOCK_Q, d).transpose(3, 0, 1, 2, 4, 5)
    key_idx = jnp.arange(s)

    def block(args):
        qi, i = args
        sc = jnp.einsum('bhcqd,bhckd->bhcqk', qi, k, preferred_element_type=jnp.float32) * scale
        q_idx = i * BLOCK_Q + jnp.arange(BLOCK_Q)
        causal = key_idx[None, :] <= q_idx[:, None]
        p = jax.nn.softmax(jnp.where(causal, sc, NEG_BIG), axis=-1)
        w = p[:, :, 0] - lam * p[:, :, 1]
        return jnp.einsum('bhqk,bhkv->bhqv', w.astype(v.dtype), v)

    out = lax.map(block, (qb, jnp.arange(nb)))
    return out.transpose(1, 2, 0, 3, 4).reshape(b, h, s, 2 * d)


def memory_cross_attention(q, k, v):
    scale = q.shape[-1] ** -0.5
    sc = jnp.einsum('bhqd,bhmd->bhqm', q, k, preferred_element_type=jnp.float32) * scale
    p = jax.nn.softmax(sc, axis=-1)
    return jnp.einsum('bhqm,bhmd->bhqd', p.astype(v.dtype), v)


def swiglu(x, w_gate_up, w_down):
    g, u = jnp.split(x @ w_gate_up, 2, axis=-1)
    return (jax.nn.silu(g) * u) @ w_down


def moe_swiglu(x, w_router, w_gate_up, w_down):
    logits = jnp.einsum('bsd,de->bse', x, w_router, preferred_element_type=jnp.float32)
    top_val, top_idx = lax.top_k(logits, TOP_K)
    top_w = jax.nn.softmax(top_val, axis=-1)
    combine = jnp.sum(jax.nn.one_hot(top_idx, N_EXPERTS, dtype=jnp.float32)
                      * top_w[..., None], axis=-2)
    out = jnp.zeros_like(x)
    for e in range(N_EXPERTS):
        out = out + combine[..., e:e + 1].astype(x.dtype) * swiglu(x, w_gate_up[e], w_down[e])
    return out


def setup_inputs(seed: int = 0) -> dict:
    key = jax.random.key(seed)
    ks = jax.random.split(key, 24)
    f32 = jnp.float32
    nrm = lambda k, shape, scale: jax.random.normal(k, shape, f32) * scale
    gain = lambda k, shape: 1.0 + 0.02 * jax.random.normal(k, shape, f32)
    x = jax.random.normal(ks[0], (BATCH, SEQ, D_MODEL), f32)
    mem = jax.random.normal(ks[1], (BATCH, MEM_LEN, D_MODEL), f32)
    offset = jax.random.randint(ks[2], (BATCH, 1), 0, POS_OFFSET_MAX, dtype=jnp.int32)
    positions = (offset + jnp.arange(SEQ, dtype=jnp.int32)[None, :]).astype(jnp.int32)
    return {
        "x": x,
        "mem": mem,
        "positions": positions,
        "attn_norm": gain(ks[3], (DEPTH, D_MODEL)),
        "w_in": nrm(ks[4], (DEPTH, D_MODEL, IN_WIDTH), D_MODEL ** -0.5),
        "w_out": nrm(ks[5], (DEPTH, D_MODEL, D_MODEL), D_MODEL ** -0.5),
        "mem_norm": gain(ks[6], (D_MODEL,)),
        "w_mem_kv": nrm(ks[7], (DEPTH, D_MODEL, 2 * MEM_WIDTH), D_MODEL ** -0.5),
        "mem_q_norm": gain(ks[8], (DEPTH, MEM_HEAD_DIM)),
        "mem_k_norm": gain(ks[9], (DEPTH, MEM_HEAD_DIM)),
        "diff_q_norm": gain(ks[10], (N_ODD, HEAD_DIM)),
        "diff_k_norm": gain(ks[11], (N_ODD, HEAD_DIM)),
        "diff_lambda": nrm(ks[12], (N_ODD, 4, HEAD_DIM), 0.1),
        "diff_subln": gain(ks[13], (N_ODD, DIFF_V_DIM)),
        "ffn_norm": gain(ks[14], (DEPTH, D_MODEL)),
        "dense_w_gate_up": nrm(ks[15], (N_EVEN, D_MODEL, 2 * D_FF_DENSE), D_MODEL ** -0.5),
        "dense_w_down": nrm(ks[16], (N_EVEN, D_FF_DENSE, D_MODEL), D_FF_DENSE ** -0.5),
        "w_router": nrm(ks[17], (N_ODD, D_MODEL, N_EXPERTS), D_MODEL ** -0.5),
        "moe_w_gate_up": nrm(ks[18], (N_ODD, N_EXPERTS, D_MODEL, 2 * D_FF_EXPERT), D_MODEL ** -0.5),
        "moe_w_down": nrm(ks[19], (N_ODD, N_EXPERTS, D_FF_EXPERT, D_MODEL), D_FF_EXPERT ** -0.5),
    }


def reference(x, mem, positions, attn_norm, w_in, w_out, mem_norm, w_mem_kv, mem_q_norm,
              mem_k_norm, diff_q_norm, diff_k_norm, diff_lambda, diff_subln, ffn_norm,
              dense_w_gate_up, dense_w_down, w_router, moe_w_gate_up, moe_w_down):
    b, s, _ = x.shape
    cos, sin = rope_tables(positions)
    cos_d, sin_d = cos[:, :, None], sin[:, :, None]
    mem_n = rms_norm(mem, mem_norm)
    for i in range(DEPTH):
        j = i // 2
        h = rms_norm(x, attn_norm[i])
        proj = h @ w_in[i]
        q_mix = proj[..., :MIX_WIDTH]
        k_mix = proj[..., MIX_WIDTH:2 * MIX_WIDTH]
        v_mix = proj[..., 2 * MIX_WIDTH:3 * MIX_WIDTH]
        q_mem = proj[..., 3 * MIX_WIDTH:]
        kv_m = mem_n @ w_mem_kv[i]
        qm = rms_norm(to_heads(q_mem, N_MEM_HEADS, MEM_HEAD_DIM), mem_q_norm[i])
        km = rms_norm(to_heads(kv_m[..., :MEM_WIDTH], N_MEM_HEADS, MEM_HEAD_DIM), mem_k_norm[i])
        vm = to_heads(kv_m[..., MEM_WIDTH:], N_MEM_HEADS, MEM_HEAD_DIM)
        o_mem = memory_cross_attention(qm, km, vm)
        if i % 2 == 0:
            q = to_heads(q_mix, N_SB_HEADS, HEAD_DIM)
            k = to_heads(k_mix, N_SB_HEADS, HEAD_DIM)
            v = to_heads(v_mix, N_SB_HEADS, HEAD_DIM)
            o_mix = stick_breaking_attention(q, k, v)
        else:
            q = q_mix.reshape(b, s, N_DIFF_HEADS, 2, HEAD_DIM).transpose(0, 2, 3, 1, 4)
            k = k_mix.reshape(b, s, N_DIFF_HEADS, 2, HEAD_DIM).transpose(0, 2, 3, 1, 4)
            q = partial_rope(rms_norm(q, diff_q_norm[j]), cos_d, sin_d)
            k = partial_rope(rms_norm(k, diff_k_norm[j]), cos_d, sin_d)
            v = to_heads(v_mix, N_DIFF_HEADS, DIFF_V_DIM)
            lam_init = 0.8 - 0.6 * math.exp(-0.3 * i)
            lp = diff_lambda[j].astype(jnp.float32)
            lam = (jnp.exp(jnp.sum(lp[0] * lp[1])) - jnp.exp(jnp.sum(lp[2] * lp[3]))
                   + lam_init)
            o = differential_attention(q, k, v, lam)
            o_mix = rms_norm(o, diff_subln[j]) * (1.0 - lam_init)
        merged = jnp.concatenate([merge_heads(o_mix), merge_heads(o_mem)], axis=-1)
        x = x + merged @ w_out[i]
        h = rms_norm(x, ffn_norm[i])
        if i % 2 == 0:
            x = x + swiglu(h, dense_w_gate_up[j], dense_w_down[j])
        else:
            x = x + moe_swiglu(h, w_router[j], moe_w_gate_up[j], moe_w_down[j])
    return x
```

```python
import functools
import math

import jax
import jax.numpy as jnp
from jax import lax
from jax.experimental import pallas as pl
from jax.experimental.pallas import tpu as pltpu

F32 = jnp.float32
BF16 = jnp.bfloat16

HEAD_DIM = 64
LANES = 128
N_MEM_HEADS = 4
ROPE_DIM = HEAD_DIM // 4
ROPE_THETA = 500000.0
N_EXPERTS = 8
EPS = 1e-6
NEG_BIG = -1e30
SCALE = HEAD_DIM ** -0.5
VMEM_LIMIT = 48 * 1024 * 1024

ROW_TILE = 512
FFN_ROW_TILE = 1024
ATTN_TILE = 256
MEM_Q_TILE = 1024


def _params(*sem):
    return pltpu.CompilerParams(dimension_semantics=sem, vmem_limit_bytes=VMEM_LIMIT)


def _split_bf16(x):
    hi = x.astype(BF16)
    lo = (x - hi.astype(F32)).astype(BF16)
    return hi, lo


def _dot(a, b):
    return jnp.dot(a, b, preferred_element_type=F32)


def _dot_nt(a, b):
    return lax.dot_general(a, b, (((1,), (1,)), ((), ())), preferred_element_type=F32)


def _dot_split(x, m):
    hi, lo = _split_bf16(x)
    return _dot(hi, m) + _dot(lo, m)


def _group_ones(n, group):
    r = lax.broadcasted_iota(jnp.int32, (n, n), 0) // group
    c = lax.broadcasted_iota(jnp.int32, (n, n), 1) // group
    return (r == c).astype(BF16)


def _head_rms(t, gain, ones):
    ss = _dot_split(t * t, ones)
    return t * lax.rsqrt(ss * (1.0 / HEAD_DIM) + EPS) * gain


def _rms(x, g):
    return x * lax.rsqrt(jnp.mean(x * x, axis=-1, keepdims=True) + EPS) * g


def _mem_kv_kernel(mem_ref, g_ref, w_ref, kg_ref, k_out, v_out):
    width = k_out.shape[-1]
    mem_n = _rms(mem_ref[...], g_ref[...]).astype(BF16)
    kv = _dot(mem_n, w_ref[0])
    ones = _group_ones(width, HEAD_DIM)
    k_out[0] = _head_rms(kv[:, :width], kg_ref[0], ones).astype(BF16)
    v_out[0] = kv[:, width:].astype(BF16)


def _mem_kv(mem2d, mem_norm, w_mem_kv, k_gain):
    depth, d_model, two_w = w_mem_kv.shape
    width = two_w // 2
    rows = mem2d.shape[0]
    out = jax.ShapeDtypeStruct((depth, rows, width), BF16)
    return pl.pallas_call(
        _mem_kv_kernel,
        out_shape=(out, out),
        grid=(depth,),
        in_specs=[
            pl.BlockSpec((rows, d_model), lambda i: (0, 0)),
            pl.BlockSpec((1, d_model), lambda i: (0, 0)),
            pl.BlockSpec((1, d_model, two_w), lambda i: (i, 0, 0)),
            pl.BlockSpec((1, 1, width), lambda i: (i, 0, 0)),
        ],
        out_specs=(pl.BlockSpec((1, rows, width), lambda i: (i, 0, 0)),
                   pl.BlockSpec((1, rows, width), lambda i: (i, 0, 0))),
        compiler_params=_params("arbitrary"),
        name="mem_kv",
    )(mem2d, mem_norm, w_mem_kv, k_gain)


def _rope(t, cos, sin_lo, sin_hi):
    n = t.shape[-1]
    half = ROPE_DIM // 2
    return t * cos + pltpu.roll(t, n - half, 1) * sin_lo + pltpu.roll(t, half, 1) * sin_hi


def _in_proj_kernel(*refs, mix_width, diff):
    if diff:
        (x_ref, g_ref, w_ref, mg_ref, qg_ref, kg_ref, cos_ref, slo_ref, shi_ref,
         q_out, k_out, v_out, m_out) = refs
    else:
        x_ref, g_ref, w_ref, mg_ref, q_out, k_out, v_out, m_out = refs
    h = _rms(x_ref[...], g_ref[...]).astype(BF16)
    ones = _group_ones(LANES, HEAD_DIM)
    chunk = 2 * LANES
    for c in range(mix_width // chunk):
        lo = c * chunk
        q = _dot(h, w_ref[:, lo:lo + chunk])
        k = _dot(h, w_ref[:, mix_width + lo:mix_width + lo + chunk])
        for half in range(2):
            sl = slice(half * LANES, (half + 1) * LANES)
            dst = slice(lo + half * LANES, lo + (half + 1) * LANES)
            qh, kh = q[:, sl], k[:, sl]
            if diff:
                rope = functools.partial(_rope, cos=cos_ref[...], sin_lo=slo_ref[...],
                                         sin_hi=shi_ref[...])
                qh = rope(_head_rms(qh, qg_ref[...], ones))
                kh = rope(_head_rms(kh, kg_ref[...], ones))
            q_out[:, dst] = (qh * SCALE).astype(BF16)
            k_out[:, dst] = kh.astype(BF16)
    v_out[...] = _dot(h, w_ref[:, 2 * mix_width:3 * mix_width]).astype(BF16)
    qm = _dot(h, w_ref[:, 3 * mix_width:])
    mem_width = qm.shape[-1]
    for c in range(mem_width // LANES):
        sl = slice(c * LANES, (c + 1) * LANES)
        m_out[:, sl] = (_head_rms(qm[:, sl], mg_ref[...], ones) * SCALE).astype(BF16)


def _in_proj(x2d, g, w, mem_q_gain, diff_args, *, mix_width):
    rows, d_model = x2d.shape
    in_width = w.shape[1]
    mem_width = in_width - 3 * mix_width
    tm = ROW_TILE
    row_spec = lambda n: pl.BlockSpec((tm, n), lambda i: (i, 0))
    const_spec = lambda a: pl.BlockSpec(a.shape, lambda i: (0, 0))
    diff = diff_args is not None
    in_specs = [row_spec(d_model), const_spec(g), const_spec(w), const_spec(mem_q_gain)]
    args = [x2d, g, w, mem_q_gain]
    if diff:
        q_gain, k_gain, cos, sin_lo, sin_hi = diff_args
        in_specs += [const_spec(q_gain), const_spec(k_gain)] + [row_spec(LANES)] * 3
        args += [q_gain, k_gain, cos, sin_lo, sin_hi]
    mix = jax.ShapeDtypeStruct((rows, mix_width), BF16)
    return pl.pallas_call(
        functools.partial(_in_proj_kernel, mix_width=mix_width, diff=diff),
        out_shape=(mix, mix, mix, jax.ShapeDtypeStruct((rows, mem_width), BF16)),
        grid=(rows // tm,),
        in_specs=in_specs,
        out_specs=(row_spec(mix_width), row_spec(mix_width), row_spec(mix_width),
                   row_spec(mem_width)),
        compiler_params=_params("parallel"),
        name="in_proj_diff" if diff else "in_proj_sb",
    )(*args)


def _mem_attn_kernel(q_ref, k_ref, v_ref, o_ref):
    q, k, v = q_ref[0], k_ref[0], v_ref[0]
    lane = lax.broadcasted_iota(jnp.int32, (1, q.shape[-1]), 1) // HEAD_DIM
    out = jnp.zeros(q.shape, F32)
    for hd in range(N_MEM_HEADS):
        sel = lane == hd
        s = _dot_nt(jnp.where(sel, q, 0), k)
        p = jnp.exp(s - jnp.max(s, axis=-1, keepdims=True))
        p = p / jnp.sum(p, axis=-1, keepdims=True)
        out = out + _dot(p.astype(BF16), jnp.where(sel, v, 0))
    o_ref[0] = out.astype(BF16)


def _mem_attn(qm, km, vm):
    b, s, width = qm.shape
    m = km.shape[1]
    tq = MEM_Q_TILE
    return pl.pallas_call(
        _mem_attn_kernel,
        out_shape=jax.ShapeDtypeStruct((b, s, width), BF16),
        grid=(b, s // tq),
        in_specs=[pl.BlockSpec((1, tq, width), lambda bi, i: (bi, i, 0)),
                  pl.BlockSpec((1, m, width), lambda bi, i: (bi, 0, 0)),
                  pl.BlockSpec((1, m, width), lambda bi, i: (bi, 0, 0))],
        out_specs=pl.BlockSpec((1, tq, width), lambda bi, i: (bi, i, 0)),
        compiler_params=_params("parallel", "parallel"),
        name="mem_attn",
    )(qm, km, vm)


def _sb_kernel(q_ref, k_ref, v_ref, o_ref):
    t = ATTN_TILE
    i = pl.program_id(2)
    q = q_ref[0]
    lane = lax.broadcasted_iota(jnp.int32, (1, LANES), 1)
    first = lane < HEAD_DIM
    q_heads = (jnp.where(first, q, 0), jnp.where(first, 0, q))
    row = lax.broadcasted_iota(jnp.int32, (t, t), 0)
    col = lax.broadcasted_iota(jnp.int32, (t, t), 1)
    later = (row > col).astype(BF16)
    strict = col < row

    def block(j, carry, mask):
        spent_a, spent_b, o = carry
        kb = k_ref[0, pl.ds(j * t, t), :]
        vb = v_ref[0, pl.ds(j * t, t), :]
        v_heads = (jnp.where(first, vb, 0), jnp.where(first, 0, vb))
        spent = [spent_a, spent_b]
        for hd in range(2):
            z = _dot_nt(q_heads[hd], kb)
            sp = jnp.maximum(z, 0.0) + jnp.log(1.0 + jnp.exp(-jnp.abs(z)))
            if mask is not None:
                sp = jnp.where(mask, sp, 0.0)
            after = _dot_split(sp, later) + spent[hd]
            w = jnp.exp(z - sp - after)
            if mask is not None:
                w = jnp.where(mask, w, 0.0)
            o = o + _dot(w.astype(BF16), v_heads[hd])
            spent[hd] = spent[hd] + jnp.sum(sp, axis=-1, keepdims=True)
        return spent[0], spent[1], o

    zero = jnp.zeros((t, 1), F32)
    carry = block(i, (zero, zero, jnp.zeros((t, LANES), F32)), strict)
    carry = lax.fori_loop(0, i, lambda n, c: block(i - 1 - n, c, None), carry)
    o_ref[0] = carry[2].astype(BF16)


def _token_attn(kernel, name, q, k, v, extra=()):
    b, s, width = q.shape
    t = ATTN_TILE
    q_spec = pl.BlockSpec((1, t, LANES), lambda bi, p, i: (bi, i, p))
    kv_spec = pl.BlockSpec((1, s, LANES), lambda bi, p, i: (bi, 0, p))
    extra_specs = [pl.BlockSpec(a.shape, lambda bi, p, i: (0, 0)) for a in extra]
    return pl.pallas_call(
        kernel,
        out_shape=jax.ShapeDtypeStruct((b, s, width), BF16),
        grid=(b, width // LANES, s // t),
        in_specs=[q_spec, kv_spec, kv_spec] + extra_specs,
        out_specs=q_spec,
        compiler_params=_params("parallel", "parallel", "arbitrary"),
        name=name,
    )(q, k, v, *extra)


def _diff_kernel(q_ref, k_ref, v_ref, lam_ref, g_ref, o_ref, *, out_scale):
    t = ATTN_TILE
    i = pl.program_id(2)
    q = q_ref[0]
    lane = lax.broadcasted_iota(jnp.int32, (1, LANES), 1)
    first = lane < HEAD_DIM
    q_maps = (jnp.where(first, q, 0), jnp.where(first, 0, q))
    row = lax.broadcasted_iota(jnp.int32, (t, t), 0)
    col = lax.broadcasted_iota(jnp.int32, (t, t), 1)
    causal = col <= row

    def block(j, carry, mask):
        kb = k_ref[0, pl.ds(j * t, t), :]
        vb = v_ref[0, pl.ds(j * t, t), :]
        new = []
        for c in range(2):
            m, l, acc = carry[c]
            s = _dot_nt(q_maps[c], kb)
            if mask is not None:
                s = jnp.where(mask, s, NEG_BIG)
            m_new = jnp.maximum(m, jnp.max(s, axis=-1, keepdims=True))
            alpha = jnp.exp(m - m_new)
            p = jnp.exp(s - m_new)
            l = alpha * l + jnp.sum(p, axis=-1, keepdims=True)
            acc = alpha * acc + _dot(p.astype(BF16), vb)
            new.append((m_new, l, acc))
        return tuple(new)

    init = (jnp.full((t, 1), NEG_BIG, F32), jnp.zeros((t, 1), F32), jnp.zeros((t, LANES), F32))
    carry = block(i, (init, init), causal)
    carry = lax.fori_loop(0, i, lambda n, c: block(n, c, None), carry)
    (_, l1, acc1), (_, l2, acc2) = carry
    o = acc1 / l1 - lam_ref[...] * (acc2 / l2)
    o_ref[0] = (_rms(o, g_ref[...]) * out_scale).astype(BF16)


def _top2_combine(logits):
    lane = lax.broadcasted_iota(jnp.int32, logits.shape, 1)
    lg = jnp.where(lane < N_EXPERTS, logits, -jnp.inf)
    m1 = jnp.max(lg, axis=-1, keepdims=True)
    i1 = jnp.min(jnp.where(lg == m1, lane, LANES), axis=-1, keepdims=True)
    lg2 = jnp.where(lane == i1, -jnp.inf, lg)
    m2 = jnp.max(lg2, axis=-1, keepdims=True)
    i2 = jnp.min(jnp.where(lg2 == m2, lane, LANES), axis=-1, keepdims=True)
    e = jnp.exp(m2 - m1)
    w1 = 1.0 / (1.0 + e)
    return jnp.where(lane == i1, w1, 0.0) + jnp.where(lane == i2, e * w1, 0.0)


def _out_proj_kernel(*refs, moe):
    if moe:
        mix_ref, mem_ref, w_ref, x_ref, g_ref, wr_ref, x_out, h_out, c_out = refs
    else:
        mix_ref, mem_ref, w_ref, x_ref, g_ref, x_out, h_out = refs
    mix_width = mix_ref.shape[-1]
    x = x_ref[...] + _dot(mix_ref[...], w_ref[:mix_width, :]) + _dot(mem_ref[...], w_ref[mix_width:, :])
    x_out[...] = x
    h = _rms(x, g_ref[...])
    h_out[...] = h.astype(BF16)
    if moe:
        h_hi, h_lo = _split_bf16(h)
        w_hi, w_lo = _split_bf16(wr_ref[...])
        c_out[...] = _top2_combine(_dot(h_hi, w_hi) + _dot(h_hi, w_lo) + _dot(h_lo, w_hi))


def _out_proj(o_mix, o_mem, w, x2d, g, w_router):
    rows, d_model = x2d.shape
    tm = ROW_TILE
    row_spec = lambda n: pl.BlockSpec((tm, n), lambda i: (i, 0))
    const_spec = lambda a: pl.BlockSpec(a.shape, lambda i: (0, 0))
    moe = w_router is not None
    in_specs = [row_spec(o_mix.shape[1]), row_spec(o_mem.shape[1]), const_spec(w),
                row_spec(d_model), const_spec(g)]
    args = [o_mix, o_mem, w, x2d, g]
    out_shape = [jax.ShapeDtypeStruct((rows, d_model), F32), jax.ShapeDtypeStruct((rows, d_model), BF16)]
    out_specs = [row_spec(d_model), row_spec(d_model)]
    if moe:
        in_specs.append(const_spec(w_router))
        args.append(w_router)
        out_shape.append(jax.ShapeDtypeStruct((rows, LANES), F32))
        out_specs.append(row_spec(LANES))
    return pl.pallas_call(
        functools.partial(_out_proj_kernel, moe=moe),
        out_shape=tuple(out_shape),
        grid=(rows // tm,),
        in_specs=in_specs,
        out_specs=tuple(out_specs),
        compiler_params=_params("parallel"),
        name="out_proj_moe" if moe else "out_proj",
    )(*args)


def _swiglu_chunk(h, wg, wu, wd):
    g = _dot(h, wg)
    u = _dot(h, wu)
    return _dot((g * jax.nn.sigmoid(g) * u).astype(BF16), wd)


def _dense_ffn_kernel(h_ref, wg_ref, wu_ref, wd_ref, x_ref, o_ref, acc_ref):
    f = pl.program_id(1)

    @pl.when(f == 0)
    def _():
        acc_ref[...] = x_ref[...]

    acc_ref[...] += _swiglu_chunk(h_ref[...], wg_ref[...], wu_ref[...], wd_ref[...])

    @pl.when(f == pl.num_programs(1) - 1)
    def _():
        o_ref[...] = acc_ref[...]


def _dense_ffn(h, w_gate_up, w_down, x2d, *, ff_tile):
    rows, d_model = x2d.shape
    d_ff = w_down.shape[0]
    nf = d_ff // ff_tile
    tm = FFN_ROW_TILE
    return pl.pallas_call(
        _dense_ffn_kernel,
        out_shape=jax.ShapeDtypeStruct((rows, d_model), F32),
        grid=(rows // tm, nf),
        in_specs=[pl.BlockSpec((tm, d_model), lambda i, f: (i, 0)),
                  pl.BlockSpec((d_model, ff_tile), lambda i, f: (0, f)),
                  pl.BlockSpec((d_model, ff_tile), lambda i, f: (0, f + nf)),
                  pl.BlockSpec((ff_tile, d_model), lambda i, f: (f, 0)),
                  pl.BlockSpec((tm, d_model), lambda i, f: (i, 0))],
        out_specs=pl.BlockSpec((tm, d_model), lambda i, f: (i, 0)),
        scratch_shapes=[pltpu.VMEM((tm, d_model), F32)],
        compiler_params=_params("parallel", "arbitrary"),
        name="dense_ffn",
    )(h, w_gate_up, w_gate_up, w_down, x2d)


def _moe_ffn_kernel(h_ref, c_ref, wg_ref, wu_ref, wd_ref, x_ref, o_ref, acc_ref):
    e, f = pl.program_id(1), pl.program_id(2)

    @pl.when((e == 0) & (f == 0))
    def _():
        acc_ref[...] = x_ref[...]

    c = c_ref[...]
    lane = lax.broadcasted_iota(jnp.int32, c.shape, 1)
    gate = jnp.sum(jnp.where(lane == e, c, 0.0), axis=-1, keepdims=True)
    acc_ref[...] += gate * _swiglu_chunk(h_ref[...], wg_ref[0], wu_ref[0], wd_ref[0])

    @pl.when((e == pl.num_programs(1) - 1) & (f == pl.num_programs(2) - 1))
    def _():
        o_ref[...] = acc_ref[...]


def _moe_ffn(h, combine, w_gate_up, w_down, x2d, *, ff_tile):
    rows, d_model = x2d.shape
    n_exp, d_ff, _ = w_down.shape
    nf = d_ff // ff_tile
    tm = FFN_ROW_TILE
    return pl.pallas_call(
        _moe_ffn_kernel,
        out_shape=jax.ShapeDtypeStruct((rows, d_model), F32),
        grid=(rows // tm, n_exp, nf),
        in_specs=[pl.BlockSpec((tm, d_model), lambda i, e, f: (i, 0)),
                  pl.BlockSpec((tm, LANES), lambda i, e, f: (i, 0)),
                  pl.BlockSpec((1, d_model, ff_tile), lambda i, e, f: (e, 0, f)),
                  pl.BlockSpec((1, d_model, ff_tile), lambda i, e, f: (e, 0, f + nf)),
                  pl.BlockSpec((1, ff_tile, d_model), lambda i, e, f: (e, f, 0)),
                  pl.BlockSpec((tm, d_model), lambda i, e, f: (i, 0))],
        out_specs=pl.BlockSpec((tm, d_model), lambda i, e, f: (i, 0)),
        scratch_shapes=[pltpu.VMEM((tm, d_model), F32)],
        compiler_params=_params("parallel", "arbitrary", "arbitrary"),
        name="moe_ffn",
    )(h, combine, w_gate_up, w_gate_up, w_down, x2d)


def _rope_tables(positions):
    half = ROPE_DIM // 2
    inv_freq = ROPE_THETA ** (-jnp.arange(0, ROPE_DIM, 2, dtype=F32) / ROPE_DIM)
    ang = positions.astype(F32).reshape(-1, 1) * inv_freq
    cos, sin = jnp.cos(ang), jnp.sin(ang)
    rows = ang.shape[0]
    pad = jnp.zeros((rows, HEAD_DIM - ROPE_DIM), F32)
    zero = jnp.zeros((rows, half), F32)
    cos_h = jnp.concatenate([cos, cos, pad + 1.0], axis=-1)
    lo_h = jnp.concatenate([-sin, zero, pad], axis=-1)
    hi_h = jnp.concatenate([zero, sin, pad], axis=-1)
    tile = lambda a: jnp.concatenate([a] * (LANES // HEAD_DIM), axis=-1)
    return tile(cos_h), tile(lo_h), tile(hi_h)


def _tile_gain(g, width):
    return jnp.tile(g.astype(F32), width // g.shape[-1]).reshape(1, width)


def _ff_tile(d_ff, limit):
    return max(t for t in range(LANES, limit + 1, LANES) if d_ff % t == 0)


def kernel(x, mem, positions, attn_norm, w_in, w_out, mem_norm, w_mem_kv, mem_q_norm, mem_k_norm,
           diff_q_norm, diff_k_norm, diff_lambda, diff_subln, ffn_norm, dense_w_gate_up,
           dense_w_down, w_router, moe_w_gate_up, moe_w_down):
    b, s, d_model = x.shape
    depth = w_in.shape[0]
    mem_len = mem.shape[1]
    mem_width = w_mem_kv.shape[-1] // 2
    mix_width = (w_in.shape[-1] - mem_width) // 3
    rows = b * s
    assert rows % FFN_ROW_TILE == 0 and s % MEM_Q_TILE == 0 and mix_width % (2 * LANES) == 0

    row = lambda a: a.astype(F32).reshape(1, -1)
    k_gain = jnp.stack([_tile_gain(mem_k_norm[i], mem_width) for i in range(depth)])
    km, vm = _mem_kv(mem.reshape(b * mem_len, d_model), row(mem_norm), w_mem_kv.astype(BF16), k_gain)
    km = km.reshape(depth, b, mem_len, mem_width)
    vm = vm.reshape(depth, b, mem_len, mem_width)
    cos, sin_lo, sin_hi = _rope_tables(positions)

    x2d = x.reshape(rows, d_model)
    for i in range(depth):
        j = i // 2
        is_diff = i % 2 == 1
        diff_args = None
        if is_diff:
            diff_args = (_tile_gain(diff_q_norm[j], LANES), _tile_gain(diff_k_norm[j], LANES),
                         cos, sin_lo, sin_hi)
        q, k, v, qm = _in_proj(x2d, row(attn_norm[i]), w_in[i].astype(BF16),
                               _tile_gain(mem_q_norm[i], LANES), diff_args, mix_width=mix_width)
        to3 = lambda a: a.reshape(b, s, a.shape[-1])
        o_mem = _mem_attn(to3(qm), km[i], vm[i])
        if is_diff:
            lam_init = 0.8 - 0.6 * math.exp(-0.3 * i)
            lp = diff_lambda[j].astype(F32)
            lam = jnp.exp(jnp.sum(lp[0] * lp[1])) - jnp.exp(jnp.sum(lp[2] * lp[3])) + lam_init
            o_mix = _token_attn(functools.partial(_diff_kernel, out_scale=1.0 - lam_init), "diff_attn",
                                to3(q), to3(k), to3(v), extra=(lam.reshape(1, 1), row(diff_subln[j])))
        else:
            o_mix = _token_attn(_sb_kernel, "sb_attn", to3(q), to3(k), to3(v))
        router = None
        if is_diff:
            router = jnp.pad(w_router[j].astype(F32), ((0, 0), (0, LANES - w_router.shape[-1])))
        outs = _out_proj(o_mix.reshape(rows, mix_width), o_mem.reshape(rows, mem_width),
                         w_out[i].astype(BF16), x2d, row(ffn_norm[i]), router)
        if is_diff:
            x2d, h, combine = outs
            x2d = _moe_ffn(h, combine, moe_w_gate_up[j].astype(BF16), moe_w_down[j].astype(BF16), x2d,
                           ff_tile=_ff_tile(moe_w_down.shape[-2], 1024))
        else:
            x2d, h = outs
            x2d = _dense_ffn(h, dense_w_gate_up[j].astype(BF16), dense_w_down[j].astype(BF16), x2d,
                             ff_tile=_ff_tile(dense_w_down.shape[-2], 1536))
    return x2d.reshape(b, s, d_model)
```

```python
import functools
import math

import jax
import jax.numpy as jnp
from jax import lax
from jax.experimental import pallas as pl
from jax.experimental.pallas import tpu as pltpu

F32 = jnp.float32
BF16 = jnp.bfloat16

HEAD_DIM = 64
LANES = 128
N_MEM_HEADS = 4
ROPE_DIM = HEAD_DIM // 4
ROPE_THETA = 500000.0
N_EXPERTS = 8
EPS = 1e-6
NEG_BIG = -1e30
SCALE = HEAD_DIM ** -0.5
SB_DONE = 104.0
VMEM_LIMIT = 48 * 1024 * 1024

ROW_TILE = 512
FFN_ROW_TILE = 1024
SB_TILE = 256
DIFF_TILE = 512
MOE_ROW_TILE = 1024
ROUTE_CHUNK = 512
MEM_Q_TILE = 1024


def _params(*sem):
    return pltpu.CompilerParams(dimension_semantics=sem, vmem_limit_bytes=VMEM_LIMIT)


def _split_bf16(x):
    hi = x.astype(BF16)
    lo = (x - hi.astype(F32)).astype(BF16)
    return hi, lo


def _dot(a, b):
    return jnp.dot(a, b, preferred_element_type=F32)


def _dot_nt(a, b):
    return lax.dot_general(a, b, (((1,), (1,)), ((), ())), preferred_element_type=F32)


def _dot_split(x, m):
    hi, lo = _split_bf16(x)
    return _dot(hi, m) + _dot(lo, m)


def _group_ones(n, group):
    r = lax.broadcasted_iota(jnp.int32, (n, n), 0) // group
    c = lax.broadcasted_iota(jnp.int32, (n, n), 1) // group
    return (r == c).astype(BF16)


def _head_rms(t, gain, ones):
    ss = _dot_split(t * t, ones)
    return t * lax.rsqrt(ss * (1.0 / HEAD_DIM) + EPS) * gain


def _rms(x, g):
    return x * lax.rsqrt(jnp.mean(x * x, axis=-1, keepdims=True) + EPS) * g


def _mem_kv_kernel(mem_ref, g_ref, w_ref, kg_ref, k_out, v_out):
    width = k_out.shape[-1]
    mem_n = _rms(mem_ref[...], g_ref[...]).astype(BF16)
    kv = _dot(mem_n, w_ref[0])
    ones = _group_ones(width, HEAD_DIM)
    k_out[0] = _head_rms(kv[:, :width], kg_ref[0], ones).astype(BF16)
    v_out[0] = kv[:, width:].astype(BF16)


def _mem_kv(mem2d, mem_norm, w_mem_kv, k_gain):
    depth, d_model, two_w = w_mem_kv.shape
    width = two_w // 2
    rows = mem2d.shape[0]
    out = jax.ShapeDtypeStruct((depth, rows, width), BF16)
    return pl.pallas_call(
        _mem_kv_kernel,
        out_shape=(out, out),
        grid=(depth,),
        in_specs=[
            pl.BlockSpec((rows, d_model), lambda i: (0, 0)),
            pl.BlockSpec((1, d_model), lambda i: (0, 0)),
            pl.BlockSpec((1, d_model, two_w), lambda i: (i, 0, 0)),
            pl.BlockSpec((1, 1, width), lambda i: (i, 0, 0)),
        ],
        out_specs=(pl.BlockSpec((1, rows, width), lambda i: (i, 0, 0)),
                   pl.BlockSpec((1, rows, width), lambda i: (i, 0, 0))),
        compiler_params=_params("arbitrary"),
        name="mem_kv",
    )(mem2d, mem_norm, w_mem_kv, k_gain)


def _rope(t, cos, sin_lo, sin_hi):
    n = t.shape[-1]
    half = ROPE_DIM // 2
    return t * cos + pltpu.roll(t, n - half, 1) * sin_lo + pltpu.roll(t, half, 1) * sin_hi


def _in_proj_kernel(*refs, mix_width, diff):
    if diff:
        (x_ref, g_ref, w_ref, mg_ref, qg_ref, kg_ref, cos_ref, slo_ref, shi_ref,
         q_out, k_out, v_out, m_out) = refs
    else:
        x_ref, g_ref, w_ref, mg_ref, q_out, k_out, v_out, m_out = refs
    h = _rms(x_ref[...], g_ref[...]).astype(BF16)
    ones = _group_ones(LANES, HEAD_DIM)
    chunk = 2 * LANES
    for c in range(mix_width // chunk):
        lo = c * chunk
        q = _dot(h, w_ref[:, lo:lo + chunk])
        k = _dot(h, w_ref[:, mix_width + lo:mix_width + lo + chunk])
        for half in range(2):
            sl = slice(half * LANES, (half + 1) * LANES)
            dst = slice(lo + half * LANES, lo + (half + 1) * LANES)
            qh, kh = q[:, sl], k[:, sl]
            if diff:
                rope = functools.partial(_rope, cos=cos_ref[...], sin_lo=slo_ref[...],
                                         sin_hi=shi_ref[...])
                qh = rope(_head_rms(qh, qg_ref[...], ones))
                kh = rope(_head_rms(kh, kg_ref[...], ones))
            q_out[:, dst] = (qh * SCALE).astype(BF16)
            k_out[:, dst] = kh.astype(BF16)
    v_out[...] = _dot(h, w_ref[:, 2 * mix_width:3 * mix_width]).astype(BF16)
    qm = _dot(h, w_ref[:, 3 * mix_width:])
    mem_width = qm.shape[-1]
    for c in range(mem_width // LANES):
        sl = slice(c * LANES, (c + 1) * LANES)
        m_out[:, sl] = (_head_rms(qm[:, sl], mg_ref[...], ones) * SCALE).astype(BF16)


def _in_proj(x2d, g, w, mem_q_gain, diff_args, *, mix_width):
    rows, d_model = x2d.shape
    in_width = w.shape[1]
    mem_width = in_width - 3 * mix_width
    tm = ROW_TILE
    row_spec = lambda n: pl.BlockSpec((tm, n), lambda i: (i, 0))
    const_spec = lambda a: pl.BlockSpec(a.shape, lambda i: (0, 0))
    diff = diff_args is not None
    in_specs = [row_spec(d_model), const_spec(g), const_spec(w), const_spec(mem_q_gain)]
    args = [x2d, g, w, mem_q_gain]
    if diff:
        q_gain, k_gain, cos, sin_lo, sin_hi = diff_args
        in_specs += [const_spec(q_gain), const_spec(k_gain)] + [row_spec(LANES)] * 3
        args += [q_gain, k_gain, cos, sin_lo, sin_hi]
    mix = jax.ShapeDtypeStruct((rows, mix_width), BF16)
    return pl.pallas_call(
        functools.partial(_in_proj_kernel, mix_width=mix_width, diff=diff),
        out_shape=(mix, mix, mix, jax.ShapeDtypeStruct((rows, mem_width), BF16)),
        grid=(rows // tm,),
        in_specs=in_specs,
        out_specs=(row_spec(mix_width), row_spec(mix_width), row_spec(mix_width),
                   row_spec(mem_width)),
        compiler_params=_params("parallel"),
        name="in_proj_diff" if diff else "in_proj_sb",
    )(*args)


def _mem_attn_kernel(q_ref, k_ref, v_ref, o_ref):
    q, k, v = q_ref[0], k_ref[0], v_ref[0]
    lane = lax.broadcasted_iota(jnp.int32, (1, q.shape[-1]), 1) // HEAD_DIM
    out = jnp.zeros(q.shape, F32)
    for hd in range(N_MEM_HEADS):
        sel = lane == hd
        s = _dot_nt(jnp.where(sel, q, 0), k)
        p = jnp.exp(s - jnp.max(s, axis=-1, keepdims=True))
        p = p / jnp.sum(p, axis=-1, keepdims=True)
        out = out + _dot(p.astype(BF16), jnp.where(sel, v, 0))
    o_ref[0] = out.astype(BF16)


def _mem_attn(qm, km, vm):
    b, s, width = qm.shape
    m = km.shape[1]
    tq = MEM_Q_TILE
    return pl.pallas_call(
        _mem_attn_kernel,
        out_shape=jax.ShapeDtypeStruct((b, s, width), BF16),
        grid=(b, s // tq),
        in_specs=[pl.BlockSpec((1, tq, width), lambda bi, i: (bi, i, 0)),
                  pl.BlockSpec((1, m, width), lambda bi, i: (bi, 0, 0)),
                  pl.BlockSpec((1, m, width), lambda bi, i: (bi, 0, 0))],
        out_specs=pl.BlockSpec((1, tq, width), lambda bi, i: (bi, i, 0)),
        compiler_params=_params("parallel", "parallel"),
        name="mem_attn",
    )(qm, km, vm)


def _sb_kernel(q_ref, k_ref, v_ref, o_ref):
    t = q_ref.shape[1]
    i = pl.program_id(2)
    q = q_ref[0]
    lane = lax.broadcasted_iota(jnp.int32, (1, LANES), 1)
    first = lane < HEAD_DIM
    q_heads = (jnp.where(first, q, 0), jnp.where(first, 0, q))
    row = lax.broadcasted_iota(jnp.int32, (t, t), 0)
    col = lax.broadcasted_iota(jnp.int32, (t, t), 1)
    later = (row > col).astype(BF16)
    strict = col < row

    def block(j, carry, mask):
        spent_a, spent_b, o = carry
        kb = k_ref[0, pl.ds(j * t, t), :]
        vb = v_ref[0, pl.ds(j * t, t), :]
        v_heads = (jnp.where(first, vb, 0), jnp.where(first, 0, vb))
        spent = [spent_a, spent_b]
        for hd in range(2):
            z = _dot_nt(q_heads[hd], kb)
            sp = jnp.maximum(z, 0.0) + jnp.log(1.0 + jnp.exp(-jnp.abs(z)))
            if mask is not None:
                sp = jnp.where(mask, sp, 0.0)
            after = _dot_split(sp, later) + spent[hd]
            w = jnp.exp(z - sp - after)
            if mask is not None:
                w = jnp.where(mask, w, 0.0)
            o = o + _dot(w.astype(BF16), v_heads[hd])
            spent[hd] = spent[hd] + jnp.sum(sp, axis=-1, keepdims=True)
        return spent[0], spent[1], o

    def stick_left(spent_a, spent_b):
        return (jnp.min(jnp.minimum(spent_a, spent_b)) < SB_DONE).astype(jnp.int32)

    def earlier(state):
        n, _, carry = state
        carry = block(i - 1 - n, carry, None)
        return n + 1, stick_left(carry[0], carry[1]), carry

    zero = jnp.zeros((t, 1), F32)
    carry = block(i, (zero, zero, jnp.zeros((t, LANES), F32)), strict)
    state = (jnp.int32(0), stick_left(carry[0], carry[1]), carry)
    state = lax.while_loop(lambda st: (st[0] < i) & (st[1] > 0), earlier, state)
    o_ref[0] = state[2][2].astype(BF16)


def _token_attn(kernel, name, t, q, k, v, extra=()):
    b, s, width = q.shape
    q_spec = pl.BlockSpec((1, t, LANES), lambda bi, p, i: (bi, i, p))
    kv_spec = pl.BlockSpec((1, s, LANES), lambda bi, p, i: (bi, 0, p))
    extra_specs = [pl.BlockSpec(a.shape, lambda bi, p, i: (0, 0)) for a in extra]
    return pl.pallas_call(
        kernel,
        out_shape=jax.ShapeDtypeStruct((b, s, width), BF16),
        grid=(b, width // LANES, s // t),
        in_specs=[q_spec, kv_spec, kv_spec] + extra_specs,
        out_specs=q_spec,
        compiler_params=_params("parallel", "parallel", "arbitrary"),
        name=name,
    )(q, k, v, *extra)


def _diff_kernel(q_ref, k_ref, v_ref, lam_ref, g_ref, o_ref, *, out_scale):
    t = q_ref.shape[1]
    i = pl.program_id(2)
    q = q_ref[0]
    lane = lax.broadcasted_iota(jnp.int32, (1, LANES), 1)
    first = lane < HEAD_DIM
    q_maps = (jnp.where(first, q, 0), jnp.where(first, 0, q))
    row = lax.broadcasted_iota(jnp.int32, (t, t), 0)
    col = lax.broadcasted_iota(jnp.int32, (t, t), 1)
    causal = col <= row

    def block(j, carry, mask):
        kb = k_ref[0, pl.ds(j * t, t), :]
        vb = v_ref[0, pl.ds(j * t, t), :]
        new = []
        for c in range(2):
            m, l, acc = carry[c]
            s = _dot_nt(q_maps[c], kb)
            if mask is not None:
                s = jnp.where(mask, s, NEG_BIG)
            m_new = jnp.maximum(m, jnp.max(s, axis=-1, keepdims=True))
            alpha = jnp.exp(m - m_new)
            p = jnp.exp(s - m_new)
            l = alpha * l + jnp.sum(p, axis=-1, keepdims=True)
            acc = alpha * acc + _dot(p.astype(BF16), vb)
            new.append((m_new, l, acc))
        return tuple(new)

    init = (jnp.full((t, 1), NEG_BIG, F32), jnp.zeros((t, 1), F32), jnp.zeros((t, LANES), F32))
    carry = block(i, (init, init), causal)
    carry = lax.fori_loop(0, i, lambda n, c: block(n, c, None), carry)
    (_, l1, acc1), (_, l2, acc2) = carry
    o = acc1 / l1 - lam_ref[...] * (acc2 / l2)
    o_ref[0] = (_rms(o, g_ref[...]) * out_scale).astype(BF16)


def _top2_route(logits):
    lane = lax.broadcasted_iota(jnp.int32, logits.shape, 1)
    lg = jnp.where(lane < N_EXPERTS, logits, -jnp.inf)
    m1 = jnp.max(lg, axis=-1, keepdims=True)
    i1 = jnp.min(jnp.where(lg == m1, lane, LANES), axis=-1, keepdims=True)
    lg2 = jnp.where(lane == i1, -jnp.inf, lg)
    m2 = jnp.max(lg2, axis=-1, keepdims=True)
    i2 = jnp.min(jnp.where(lg2 == m2, lane, LANES), axis=-1, keepdims=True)
    e = jnp.exp(m2 - m1)
    w1 = 1.0 / (1.0 + e)
    fields = (i1.astype(F32), i2.astype(F32), w1, e * w1)
    out = jnp.zeros(logits.shape, F32)
    for n, val in enumerate(fields):
        out = jnp.where(lane == n, val, out)
    return out


def _out_proj_kernel(*refs, moe):
    if moe:
        mix_ref, mem_ref, w_ref, x_ref, g_ref, wr_ref, x_out, h_out, c_out = refs
    else:
        mix_ref, mem_ref, w_ref, x_ref, g_ref, x_out, h_out = refs
    mix_width = mix_ref.shape[-1]
    x = x_ref[...] + _dot(mix_ref[...], w_ref[:mix_width, :]) + _dot(mem_ref[...], w_ref[mix_width:, :])
    x_out[...] = x
    h = _rms(x, g_ref[...])
    h_out[...] = h.astype(h_out.dtype)
    if moe:
        h_hi, h_lo = _split_bf16(h)
        w_hi, w_lo = _split_bf16(wr_ref[...])
        c_out[...] = _top2_route(_dot(h_hi, w_hi) + _dot(h_hi, w_lo) + _dot(h_lo, w_hi))


def _out_proj(o_mix, o_mem, w, x2d, g, w_router):
    rows, d_model = x2d.shape
    tm = ROW_TILE
    row_spec = lambda n: pl.BlockSpec((tm, n), lambda i: (i, 0))
    const_spec = lambda a: pl.BlockSpec(a.shape, lambda i: (0, 0))
    moe = w_router is not None
    in_specs = [row_spec(o_mix.shape[1]), row_spec(o_mem.shape[1]), const_spec(w),
                row_spec(d_model), const_spec(g)]
    args = [o_mix, o_mem, w, x2d, g]
    out_shape = [jax.ShapeDtypeStruct((rows, d_model), F32),
                 jax.ShapeDtypeStruct((rows, d_model), F32 if moe else BF16)]
    out_specs = [row_spec(d_model), row_spec(d_model)]
    if moe:
        in_specs.append(const_spec(w_router))
        args.append(w_router)
        out_shape.append(jax.ShapeDtypeStruct((rows, LANES), F32))
        out_specs.append(row_spec(LANES))
    return pl.pallas_call(
        functools.partial(_out_proj_kernel, moe=moe),
        out_shape=tuple(out_shape),
        grid=(rows // tm,),
        in_specs=in_specs,
        out_specs=tuple(out_specs),
        compiler_params=_params("parallel"),
        name="out_proj_moe" if moe else "out_proj",
    )(*args)


def _swiglu_chunk(h, wg, wu, wd):
    g = _dot(h, wg)
    u = _dot(h, wu)
    return _dot((g * jax.nn.sigmoid(g) * u).astype(BF16), wd)


def _dense_ffn_kernel(h_ref, wg_ref, wu_ref, wd_ref, x_ref, o_ref, acc_ref):
    f = pl.program_id(1)

    @pl.when(f == 0)
    def _():
        acc_ref[...] = x_ref[...]

    acc_ref[...] += _swiglu_chunk(h_ref[...], wg_ref[...], wu_ref[...], wd_ref[...])

    @pl.when(f == pl.num_programs(1) - 1)
    def _():
        o_ref[...] = acc_ref[...]


def _dense_ffn(h, w_gate_up, w_down, x2d, *, ff_tile):
    rows, d_model = x2d.shape
    d_ff = w_down.shape[0]
    nf = d_ff // ff_tile
    tm = FFN_ROW_TILE
    return pl.pallas_call(
        _dense_ffn_kernel,
        out_shape=jax.ShapeDtypeStruct((rows, d_model), F32),
        grid=(rows // tm, nf),
        in_specs=[pl.BlockSpec((tm, d_model), lambda i, f: (i, 0)),
                  pl.BlockSpec((d_model, ff_tile), lambda i, f: (0, f)),
                  pl.BlockSpec((d_model, ff_tile), lambda i, f: (0, f + nf)),
                  pl.BlockSpec((ff_tile, d_model), lambda i, f: (f, 0)),
                  pl.BlockSpec((tm, d_model), lambda i, f: (i, 0))],
        out_specs=pl.BlockSpec((tm, d_model), lambda i, f: (i, 0)),
        scratch_shapes=[pltpu.VMEM((tm, d_model), F32)],
        compiler_params=_params("parallel", "arbitrary"),
        name="dense_ffn",
    )(h, w_gate_up, w_gate_up, w_down, x2d)


def _route_plan(route, tm):
    n_tokens = route.shape[0]
    expert = jnp.concatenate([route[:, 0], route[:, 1]]).astype(jnp.int32)
    onehot = (expert[:, None] == jnp.arange(N_EXPERTS, dtype=jnp.int32)[None, :]).astype(jnp.int32)
    csum = jnp.cumsum(onehot, axis=0)
    tiles = (csum[-1] + tm - 1) // tm
    tile_end = jnp.cumsum(tiles)
    start = (tile_end - tiles) * tm
    pos = jnp.sum(onehot * (start[None, :] + csum - 1), axis=1)
    n_tiles = 2 * n_tokens // tm + N_EXPERTS
    tile_expert = jnp.sum(jnp.arange(n_tiles, dtype=jnp.int32)[:, None] >= tile_end[None, :], axis=1)
    n_used = tile_end[-1]
    tile_expert = jnp.minimum(tile_expert, tile_expert[n_used - 1])
    return pos.astype(jnp.int32), jnp.concatenate([tile_expert, n_used[None]]).astype(jnp.int32)


def _dispatch_kernel(pos_ref, h_ref, init_ref, hs_ref, sem, *, n_tokens):
    del init_ref
    chunk = pos_ref.shape[-1]
    base = (pl.program_id(0) * chunk) % n_tokens

    def row_copy(r):
        return pltpu.make_async_copy(h_ref.at[pl.ds(base + r, 1)],
                                     hs_ref.at[pl.ds(pos_ref[0, 0, r], 1)], sem)

    def start(r, c):
        row_copy(r).start()
        return c

    def wait(r, c):
        row_copy(r).wait()
        return c

    lax.fori_loop(0, chunk, start, 0)
    lax.fori_loop(0, chunk, wait, 0)


def _dispatch(h, pos, n_rows):
    n_tokens, d_model = h.shape
    chunk = ROUTE_CHUNK
    n_chunks = pos.shape[0] // chunk
    return pl.pallas_call(
        functools.partial(_dispatch_kernel, n_tokens=n_tokens),
        out_shape=jax.ShapeDtypeStruct((n_rows, d_model), h.dtype),
        grid=(n_chunks,),
        in_specs=[pl.BlockSpec((1, 1, chunk), lambda c: (c, 0, 0), memory_space=pltpu.SMEM),
                  pl.BlockSpec(memory_space=pl.ANY),
                  pl.BlockSpec(memory_space=pl.ANY)],
        out_specs=pl.BlockSpec(memory_space=pl.ANY),
        scratch_shapes=[pltpu.SemaphoreType.DMA],
        input_output_aliases={2: 0},
        compiler_params=_params("arbitrary"),
        name="moe_dispatch",
    )(pos.reshape(n_chunks, 1, chunk), h, jnp.zeros((n_rows, d_model), h.dtype))


def _expert_ffn_kernel(plan_ref, hs_ref, wg_ref, wu_ref, wd_ref, ys_ref, hb_ref, acc_ref):
    i, f = pl.program_id(0), pl.program_id(1)
    last = pl.num_programs(1) - 1
    used = i < plan_ref[pl.num_programs(0)]

    @pl.when(used & (f == 0))
    def _():
        hb_ref[...] = hs_ref[...].astype(BF16)
        acc_ref[...] = jnp.zeros_like(acc_ref)

    @pl.when(used)
    def _():
        acc_ref[...] += _swiglu_chunk(hb_ref[...], wg_ref[0], wu_ref[0], wd_ref[0])

    @pl.when(used & (f == last))
    def _():
        ys_ref[...] = acc_ref[...]

    @pl.when(jnp.logical_not(used) & (f == last))
    def _():
        ys_ref[...] = jnp.zeros_like(ys_ref)


def _expert_ffn(hs, plan, w_gate_up, w_down, *, ff_tile):
    n_rows, d_model = hs.shape
    d_ff = w_down.shape[1]
    nf = d_ff // ff_tile
    tm = MOE_ROW_TILE
    n_tiles = n_rows // tm

    def chunk_of(i, f, plan_ref):
        return jnp.where(i < plan_ref[n_tiles], f, nf - 1)

    grid_spec = pltpu.PrefetchScalarGridSpec(
        num_scalar_prefetch=1,
        grid=(n_tiles, nf),
        in_specs=[
            pl.BlockSpec((tm, d_model), lambda i, f, p: (i, 0)),
            pl.BlockSpec((1, d_model, ff_tile), lambda i, f, p: (p[i], 0, chunk_of(i, f, p))),
            pl.BlockSpec((1, d_model, ff_tile), lambda i, f, p: (p[i], 0, chunk_of(i, f, p) + nf)),
            pl.BlockSpec((1, ff_tile, d_model), lambda i, f, p: (p[i], chunk_of(i, f, p), 0)),
        ],
        out_specs=pl.BlockSpec((tm, d_model), lambda i, f, p: (i, 0)),
        scratch_shapes=[pltpu.VMEM((tm, d_model), BF16), pltpu.VMEM((tm, d_model), F32)],
    )
    return pl.pallas_call(
        _expert_ffn_kernel,
        out_shape=jax.ShapeDtypeStruct((n_rows, d_model), F32),
        grid_spec=grid_spec,
        compiler_params=_params("arbitrary", "arbitrary"),
        name="moe_experts",
    )(plan, hs, w_gate_up, w_gate_up, w_down)


def _combine_kernel(p1_ref, p2_ref, ys_ref, route_ref, x_ref, o_ref, buf_ref, sem):
    chunk = x_ref.shape[0]

    def row_copy(k, pos_ref, r):
        return pltpu.make_async_copy(ys_ref.at[pl.ds(pos_ref[0, 0, r], 1)],
                                     buf_ref.at[k, pl.ds(r, 1)], sem.at[k])

    def start(r, c):
        row_copy(0, p1_ref, r).start()
        row_copy(1, p2_ref, r).start()
        return c

    def wait(r, c):
        row_copy(0, p1_ref, r).wait()
        row_copy(1, p2_ref, r).wait()
        return c

    lax.fori_loop(0, chunk, start, 0)
    lax.fori_loop(0, chunk, wait, 0)
    route = route_ref[...]
    o_ref[...] = x_ref[...] + route[:, 2:3] * buf_ref[0] + route[:, 3:4] * buf_ref[1]


def _combine(ys, pos, route, x2d):
    rows, d_model = x2d.shape
    chunk = ROUTE_CHUNK
    n_chunks = rows // chunk
    pos3 = pos.reshape(2 * n_chunks, 1, chunk)
    smem_spec = lambda off: pl.BlockSpec((1, 1, chunk), lambda c: (c + off, 0, 0), memory_space=pltpu.SMEM)
    row_spec = lambda n: pl.BlockSpec((chunk, n), lambda c: (c, 0))
    return pl.pallas_call(
        _combine_kernel,
        out_shape=jax.ShapeDtypeStruct((rows, d_model), F32),
        grid=(n_chunks,),
        in_specs=[smem_spec(0), smem_spec(n_chunks), pl.BlockSpec(memory_space=pl.ANY),
                  row_spec(LANES), row_spec(d_model)],
        out_specs=row_spec(d_model),
        scratch_shapes=[pltpu.VMEM((2, chunk, d_model), F32), pltpu.SemaphoreType.DMA((2,))],
        compiler_params=_params("arbitrary"),
        name="moe_combine",
    )(pos3, pos3, ys, route, x2d)


def _moe_ffn(h, route, w_gate_up, w_down, x2d, *, ff_tile):
    pos, plan = _route_plan(route, MOE_ROW_TILE)
    n_rows = (plan.shape[0] - 1) * MOE_ROW_TILE
    hs = _dispatch(h, pos, n_rows)
    ys = _expert_ffn(hs, plan, w_gate_up, w_down, ff_tile=ff_tile)
    return _combine(ys, pos, route, x2d)


def _rope_tables(positions):
    half = ROPE_DIM // 2
    inv_freq = ROPE_THETA ** (-jnp.arange(0, ROPE_DIM, 2, dtype=F32) / ROPE_DIM)
    ang = positions.astype(F32).reshape(-1, 1) * inv_freq
    cos, sin = jnp.cos(ang), jnp.sin(ang)
    rows = ang.shape[0]
    pad = jnp.zeros((rows, HEAD_DIM - ROPE_DIM), F32)
    zero = jnp.zeros((rows, half), F32)
    cos_h = jnp.concatenate([cos, cos, pad + 1.0], axis=-1)
    lo_h = jnp.concatenate([-sin, zero, pad], axis=-1)
    hi_h = jnp.concatenate([zero, sin, pad], axis=-1)
    tile = lambda a: jnp.concatenate([a] * (LANES // HEAD_DIM), axis=-1)
    return tile(cos_h), tile(lo_h), tile(hi_h)


def _tile_gain(g, width):
    return jnp.tile(g.astype(F32), width // g.shape[-1]).reshape(1, width)


def _ff_tile(d_ff, limit):
    return max(t for t in range(LANES, limit + 1, LANES) if d_ff % t == 0)


def kernel(x, mem, positions, attn_norm, w_in, w_out, mem_norm, w_mem_kv, mem_q_norm, mem_k_norm,
           diff_q_norm, diff_k_norm, diff_lambda, diff_subln, ffn_norm, dense_w_gate_up,
           dense_w_down, w_router, moe_w_gate_up, moe_w_down):
    b, s, d_model = x.shape
    depth = w_in.shape[0]
    mem_len = mem.shape[1]
    mem_width = w_mem_kv.shape[-1] // 2
    mix_width = (w_in.shape[-1] - mem_width) // 3
    rows = b * s
    assert rows % FFN_ROW_TILE == 0 and s % MEM_Q_TILE == 0 and mix_width % (2 * LANES) == 0

    row = lambda a: a.astype(F32).reshape(1, -1)
    k_gain = jnp.stack([_tile_gain(mem_k_norm[i], mem_width) for i in range(depth)])
    km, vm = _mem_kv(mem.reshape(b * mem_len, d_model), row(mem_norm), w_mem_kv.astype(BF16), k_gain)
    km = km.reshape(depth, b, mem_len, mem_width)
    vm = vm.reshape(depth, b, mem_len, mem_width)
    cos, sin_lo, sin_hi = _rope_tables(positions)

    x2d = x.reshape(rows, d_model)
    for i in range(depth):
        j = i // 2
        is_diff = i % 2 == 1
        diff_args = None
        if is_diff:
            diff_args = (_tile_gain(diff_q_norm[j], LANES), _tile_gain(diff_k_norm[j], LANES),
                         cos, sin_lo, sin_hi)
        q, k, v, qm = _in_proj(x2d, row(attn_norm[i]), w_in[i].astype(BF16),
                               _tile_gain(mem_q_norm[i], LANES), diff_args, mix_width=mix_width)
        to3 = lambda a: a.reshape(b, s, a.shape[-1])
        o_mem = _mem_attn(to3(qm), km[i], vm[i])
        if is_diff:
            lam_init = 0.8 - 0.6 * math.exp(-0.3 * i)
            lp = diff_lambda[j].astype(F32)
            lam = jnp.exp(jnp.sum(lp[0] * lp[1])) - jnp.exp(jnp.sum(lp[2] * lp[3])) + lam_init
            o_mix = _token_attn(functools.partial(_diff_kernel, out_scale=1.0 - lam_init), "diff_attn",
                                DIFF_TILE, to3(q), to3(k), to3(v),
                                extra=(lam.reshape(1, 1), row(diff_subln[j])))
        else:
            o_mix = _token_attn(_sb_kernel, "sb_attn", SB_TILE, to3(q), to3(k), to3(v))
        router = None
        if is_diff:
            router = jnp.pad(w_router[j].astype(F32), ((0, 0), (0, LANES - w_router.shape[-1])))
        outs = _out_proj(o_mix.reshape(rows, mix_width), o_mem.reshape(rows, mem_width),
                         w_out[i].astype(BF16), x2d, row(ffn_norm[i]), router)
        if is_diff:
            x2d, h, route = outs
            x2d = _moe_ffn(h, route, moe_w_gate_up[j].astype(BF16), moe_w_down[j].astype(BF16), x2d,
                           ff_tile=_ff_tile(moe_w_down.shape[-2], 1024))
        else:
            x2d, h = outs
            x2d = _dense_ffn(h, dense_w_gate_up[j].astype(BF16), dense_w_down[j].astype(BF16), x2d,
                             ff_tile=_ff_tile(dense_w_down.shape[-2], 1536))
    return x2d.reshape(b, s, d_model)
```

```python
import functools
import math

import jax
import jax.numpy as jnp
from jax import lax
from jax.experimental import pallas as pl
from jax.experimental.pallas import tpu as pltpu

F32 = jnp.float32
BF16 = jnp.bfloat16

HEAD_DIM = 64
LANES = 128
N_MEM_HEADS = 4
ROPE_DIM = HEAD_DIM // 4
ROPE_THETA = 500000.0
N_EXPERTS = 8
EPS = 1e-6
NEG_BIG = -1e30
SCALE = HEAD_DIM ** -0.5
SB_DONE = 104.0
VMEM_LIMIT = 48 * 1024 * 1024

ROW_TILE = 512
FFN_ROW_TILE = 1024
SB_TILE = 256
DIFF_TILE = 512
MOE_ROW_TILE = 1024
ROUTE_CHUNK = 512
DMA_UNROLL = 8
MEM_Q_TILE = 1024


def _params(*sem):
    return pltpu.CompilerParams(dimension_semantics=sem, vmem_limit_bytes=VMEM_LIMIT)


def _split_bf16(x):
    hi = x.astype(BF16)
    lo = (x - hi.astype(F32)).astype(BF16)
    return hi, lo


def _dot(a, b):
    return jnp.dot(a, b, preferred_element_type=F32)


def _dot_nt(a, b):
    return lax.dot_general(a, b, (((1,), (1,)), ((), ())), preferred_element_type=F32)


def _dot_split(x, m):
    hi, lo = _split_bf16(x)
    return _dot(hi, m) + _dot(lo, m)


def _group_ones(n, group):
    r = lax.broadcasted_iota(jnp.int32, (n, n), 0) // group
    c = lax.broadcasted_iota(jnp.int32, (n, n), 1) // group
    return (r == c).astype(BF16)


def _head_rms(t, gain, ones):
    ss = _dot_split(t * t, ones)
    return t * lax.rsqrt(ss * (1.0 / HEAD_DIM) + EPS) * gain


def _rms(x, g):
    return x * lax.rsqrt(jnp.mean(x * x, axis=-1, keepdims=True) + EPS) * g


def _mem_kv_kernel(mem_ref, g_ref, w_ref, kg_ref, k_out, v_out):
    width = k_out.shape[-1]
    mem_n = _rms(mem_ref[...], g_ref[...]).astype(BF16)
    kv = _dot(mem_n, w_ref[0])
    ones = _group_ones(width, HEAD_DIM)
    k_out[0] = _head_rms(kv[:, :width], kg_ref[0], ones).astype(BF16)
    v_out[0] = kv[:, width:].astype(BF16)


def _mem_kv(mem2d, mem_norm, w_mem_kv, k_gain):
    depth, d_model, two_w = w_mem_kv.shape
    width = two_w // 2
    rows = mem2d.shape[0]
    out = jax.ShapeDtypeStruct((depth, rows, width), BF16)
    return pl.pallas_call(
        _mem_kv_kernel,
        out_shape=(out, out),
        grid=(depth,),
        in_specs=[
            pl.BlockSpec((rows, d_model), lambda i: (0, 0)),
            pl.BlockSpec((1, d_model), lambda i: (0, 0)),
            pl.BlockSpec((1, d_model, two_w), lambda i: (i, 0, 0)),
            pl.BlockSpec((1, 1, width), lambda i: (i, 0, 0)),
        ],
        out_specs=(pl.BlockSpec((1, rows, width), lambda i: (i, 0, 0)),
                   pl.BlockSpec((1, rows, width), lambda i: (i, 0, 0))),
        compiler_params=_params("arbitrary"),
        name="mem_kv",
    )(mem2d, mem_norm, w_mem_kv, k_gain)


def _rope(t, cos, sin_lo, sin_hi):
    n = t.shape[-1]
    half = ROPE_DIM // 2
    return t * cos + pltpu.roll(t, n - half, 1) * sin_lo + pltpu.roll(t, half, 1) * sin_hi


def _in_proj_kernel(*refs, mix_width, diff):
    if diff:
        (x_ref, g_ref, w_ref, mg_ref, qg_ref, kg_ref, cos_ref, slo_ref, shi_ref,
         q_out, k_out, v_out, m_out) = refs
    else:
        x_ref, g_ref, w_ref, mg_ref, q_out, k_out, v_out, m_out = refs
    h = _rms(x_ref[...], g_ref[...]).astype(BF16)
    ones = _group_ones(LANES, HEAD_DIM)
    chunk = 2 * LANES
    for c in range(mix_width // chunk):
        lo = c * chunk
        q = _dot(h, w_ref[:, lo:lo + chunk])
        k = _dot(h, w_ref[:, mix_width + lo:mix_width + lo + chunk])
        for half in range(2):
            sl = slice(half * LANES, (half + 1) * LANES)
            dst = slice(lo + half * LANES, lo + (half + 1) * LANES)
            qh, kh = q[:, sl], k[:, sl]
            if diff:
                rope = functools.partial(_rope, cos=cos_ref[...], sin_lo=slo_ref[...],
                                         sin_hi=shi_ref[...])
                qh = rope(_head_rms(qh, qg_ref[...], ones))
                kh = rope(_head_rms(kh, kg_ref[...], ones))
            q_out[:, dst] = (qh * SCALE).astype(BF16)
            k_out[:, dst] = kh.astype(BF16)
    v_out[...] = _dot(h, w_ref[:, 2 * mix_width:3 * mix_width]).astype(BF16)
    qm = _dot(h, w_ref[:, 3 * mix_width:])
    mem_width = qm.shape[-1]
    for c in range(mem_width // LANES):
        sl = slice(c * LANES, (c + 1) * LANES)
        m_out[:, sl] = (_head_rms(qm[:, sl], mg_ref[...], ones) * SCALE).astype(BF16)


def _in_proj(x2d, g, w, mem_q_gain, diff_args, *, mix_width):
    rows, d_model = x2d.shape
    in_width = w.shape[1]
    mem_width = in_width - 3 * mix_width
    tm = ROW_TILE
    row_spec = lambda n: pl.BlockSpec((tm, n), lambda i: (i, 0))
    const_spec = lambda a: pl.BlockSpec(a.shape, lambda i: (0, 0))
    diff = diff_args is not None
    in_specs = [row_spec(d_model), const_spec(g), const_spec(w), const_spec(mem_q_gain)]
    args = [x2d, g, w, mem_q_gain]
    if diff:
        q_gain, k_gain, cos, sin_lo, sin_hi = diff_args
        in_specs += [const_spec(q_gain), const_spec(k_gain)] + [row_spec(LANES)] * 3
        args += [q_gain, k_gain, cos, sin_lo, sin_hi]
    mix = jax.ShapeDtypeStruct((rows, mix_width), BF16)
    return pl.pallas_call(
        functools.partial(_in_proj_kernel, mix_width=mix_width, diff=diff),
        out_shape=(mix, mix, mix, jax.ShapeDtypeStruct((rows, mem_width), BF16)),
        grid=(rows // tm,),
        in_specs=in_specs,
        out_specs=(row_spec(mix_width), row_spec(mix_width), row_spec(mix_width),
                   row_spec(mem_width)),
        compiler_params=_params("parallel"),
        name="in_proj_diff" if diff else "in_proj_sb",
    )(*args)


def _mem_attn_kernel(q_ref, k_ref, v_ref, o_ref):
    q, k, v = q_ref[0], k_ref[0], v_ref[0]
    lane = lax.broadcasted_iota(jnp.int32, (1, q.shape[-1]), 1) // HEAD_DIM
    out = jnp.zeros(q.shape, F32)
    for hd in range(N_MEM_HEADS):
        sel = lane == hd
        s = _dot_nt(jnp.where(sel, q, 0), k)
        p = jnp.exp(s - jnp.max(s, axis=-1, keepdims=True))
        p = p / jnp.sum(p, axis=-1, keepdims=True)
        out = out + _dot(p.astype(BF16), jnp.where(sel, v, 0))
    o_ref[0] = out.astype(BF16)


def _mem_attn(qm, km, vm):
    b, s, width = qm.shape
    m = km.shape[1]
    tq = MEM_Q_TILE
    return pl.pallas_call(
        _mem_attn_kernel,
        out_shape=jax.ShapeDtypeStruct((b, s, width), BF16),
        grid=(b, s // tq),
        in_specs=[pl.BlockSpec((1, tq, width), lambda bi, i: (bi, i, 0)),
                  pl.BlockSpec((1, m, width), lambda bi, i: (bi, 0, 0)),
                  pl.BlockSpec((1, m, width), lambda bi, i: (bi, 0, 0))],
        out_specs=pl.BlockSpec((1, tq, width), lambda bi, i: (bi, i, 0)),
        compiler_params=_params("parallel", "parallel"),
        name="mem_attn",
    )(qm, km, vm)


def _sb_kernel(q_ref, k_ref, v_ref, o_ref):
    t = q_ref.shape[1]
    i = pl.program_id(2)
    q = q_ref[0]
    lane = lax.broadcasted_iota(jnp.int32, (1, LANES), 1)
    first = lane < HEAD_DIM
    q_heads = (jnp.where(first, q, 0), jnp.where(first, 0, q))
    row = lax.broadcasted_iota(jnp.int32, (t, t), 0)
    col = lax.broadcasted_iota(jnp.int32, (t, t), 1)
    later = (row > col).astype(BF16)
    strict = col < row

    def block(j, carry, mask):
        spent_a, spent_b, o = carry
        kb = k_ref[0, pl.ds(j * t, t), :]
        vb = v_ref[0, pl.ds(j * t, t), :]
        v_heads = (jnp.where(first, vb, 0), jnp.where(first, 0, vb))
        spent = [spent_a, spent_b]
        for hd in range(2):
            z = _dot_nt(q_heads[hd], kb)
            sp = jnp.maximum(z, 0.0) + jnp.log(1.0 + jnp.exp(-jnp.abs(z)))
            if mask is not None:
                sp = jnp.where(mask, sp, 0.0)
            after = _dot_split(sp, later) + spent[hd]
            w = jnp.exp(z - sp - after)
            if mask is not None:
                w = jnp.where(mask, w, 0.0)
            o = o + _dot(w.astype(BF16), v_heads[hd])
            spent[hd] = spent[hd] + jnp.sum(sp, axis=-1, keepdims=True)
        return spent[0], spent[1], o

    def stick_left(spent_a, spent_b):
        return (jnp.min(jnp.minimum(spent_a, spent_b)) < SB_DONE).astype(jnp.int32)

    def earlier(state):
        n, _, carry = state
        carry = block(i - 1 - n, carry, None)
        return n + 1, stick_left(carry[0], carry[1]), carry

    zero = jnp.zeros((t, 1), F32)
    carry = block(i, (zero, zero, jnp.zeros((t, LANES), F32)), strict)
    state = (jnp.int32(0), stick_left(carry[0], carry[1]), carry)
    state = lax.while_loop(lambda st: (st[0] < i) & (st[1] > 0), earlier, state)
    o_ref[0] = state[2][2].astype(BF16)


def _token_attn(kernel, name, t, q, k, v, extra=()):
    b, s, width = q.shape
    q_spec = pl.BlockSpec((1, t, LANES), lambda bi, p, i: (bi, i, p))
    kv_spec = pl.BlockSpec((1, s, LANES), lambda bi, p, i: (bi, 0, p))
    extra_specs = [pl.BlockSpec(a.shape, lambda bi, p, i: (0, 0)) for a in extra]
    return pl.pallas_call(
        kernel,
        out_shape=jax.ShapeDtypeStruct((b, s, width), BF16),
        grid=(b, width // LANES, s // t),
        in_specs=[q_spec, kv_spec, kv_spec] + extra_specs,
        out_specs=q_spec,
        compiler_params=_params("parallel", "parallel", "arbitrary"),
        name=name,
    )(q, k, v, *extra)


def _diff_kernel(q_ref, k_ref, v_ref, lam_ref, g_ref, o_ref, *, out_scale):
    t = q_ref.shape[1]
    i = pl.program_id(2)
    q = q_ref[0]
    lane = lax.broadcasted_iota(jnp.int32, (1, LANES), 1)
    first = lane < HEAD_DIM
    q_maps = (jnp.where(first, q, 0), jnp.where(first, 0, q))
    row = lax.broadcasted_iota(jnp.int32, (t, t), 0)
    col = lax.broadcasted_iota(jnp.int32, (t, t), 1)
    causal = col <= row

    def block(j, carry, mask):
        kb = k_ref[0, pl.ds(j * t, t), :]
        vb = v_ref[0, pl.ds(j * t, t), :]
        new = []
        for c in range(2):
            m, l, acc = carry[c]
            s = _dot_nt(q_maps[c], kb)
            if mask is not None:
                s = jnp.where(mask, s, NEG_BIG)
            m_new = jnp.maximum(m, jnp.max(s, axis=-1, keepdims=True))
            alpha = jnp.exp(m - m_new)
            p = jnp.exp(s - m_new)
            l = alpha * l + jnp.sum(p, axis=-1, keepdims=True)
            acc = alpha * acc + _dot(p.astype(BF16), vb)
            new.append((m_new, l, acc))
        return tuple(new)

    init = (jnp.full((t, 1), NEG_BIG, F32), jnp.zeros((t, 1), F32), jnp.zeros((t, LANES), F32))
    carry = block(i, (init, init), causal)
    carry = lax.fori_loop(0, i, lambda n, c: block(n, c, None), carry)
    (_, l1, acc1), (_, l2, acc2) = carry
    o = acc1 / l1 - lam_ref[...] * (acc2 / l2)
    o_ref[0] = (_rms(o, g_ref[...]) * out_scale).astype(BF16)


def _top2_route(logits):
    lane = lax.broadcasted_iota(jnp.int32, logits.shape, 1)
    lg = jnp.where(lane < N_EXPERTS, logits, -jnp.inf)
    m1 = jnp.max(lg, axis=-1, keepdims=True)
    i1 = jnp.min(jnp.where(lg == m1, lane, LANES), axis=-1, keepdims=True)
    lg2 = jnp.where(lane == i1, -jnp.inf, lg)
    m2 = jnp.max(lg2, axis=-1, keepdims=True)
    i2 = jnp.min(jnp.where(lg2 == m2, lane, LANES), axis=-1, keepdims=True)
    e = jnp.exp(m2 - m1)
    w1 = 1.0 / (1.0 + e)
    fields = (i1.astype(F32), i2.astype(F32), w1, e * w1)
    out = jnp.zeros(logits.shape, F32)
    for n, val in enumerate(fields):
        out = jnp.where(lane == n, val, out)
    return out


def _out_proj_kernel(*refs, moe):
    if moe:
        mix_ref, mem_ref, w_ref, x_ref, g_ref, wr_ref, x_out, h_out, c_out = refs
    else:
        mix_ref, mem_ref, w_ref, x_ref, g_ref, x_out, h_out = refs
    mix_width = mix_ref.shape[-1]
    x = x_ref[...] + _dot(mix_ref[...], w_ref[:mix_width, :]) + _dot(mem_ref[...], w_ref[mix_width:, :])
    x_out[...] = x
    h = _rms(x, g_ref[...])
    h_out[...] = h.astype(h_out.dtype)
    if moe:
        h_hi, h_lo = _split_bf16(h)
        w_hi, w_lo = _split_bf16(wr_ref[...])
        c_out[...] = _top2_route(_dot(h_hi, w_hi) + _dot(h_hi, w_lo) + _dot(h_lo, w_hi))


def _out_proj(o_mix, o_mem, w, x2d, g, w_router):
    rows, d_model = x2d.shape
    tm = ROW_TILE
    row_spec = lambda n: pl.BlockSpec((tm, n), lambda i: (i, 0))
    const_spec = lambda a: pl.BlockSpec(a.shape, lambda i: (0, 0))
    moe = w_router is not None
    in_specs = [row_spec(o_mix.shape[1]), row_spec(o_mem.shape[1]), const_spec(w),
                row_spec(d_model), const_spec(g)]
    args = [o_mix, o_mem, w, x2d, g]
    out_shape = [jax.ShapeDtypeStruct((rows, d_model), F32),
                 jax.ShapeDtypeStruct((rows, d_model), F32 if moe else BF16)]
    out_specs = [row_spec(d_model), row_spec(d_model)]
    if moe:
        in_specs.append(const_spec(w_router))
        args.append(w_router)
        out_shape.append(jax.ShapeDtypeStruct((rows, LANES), F32))
        out_specs.append(row_spec(LANES))
    return pl.pallas_call(
        functools.partial(_out_proj_kernel, moe=moe),
        out_shape=tuple(out_shape),
        grid=(rows // tm,),
        in_specs=in_specs,
        out_specs=tuple(out_specs),
        compiler_params=_params("parallel"),
        name="out_proj_moe" if moe else "out_proj",
    )(*args)


def _swiglu_chunk(h, wg, wu, wd):
    g = _dot(h, wg)
    u = _dot(h, wu)
    return _dot((g * jax.nn.sigmoid(g) * u).astype(BF16), wd)


def _dense_ffn_kernel(h_ref, wg_ref, wu_ref, wd_ref, x_ref, o_ref, acc_ref):
    f = pl.program_id(1)

    @pl.when(f == 0)
    def _():
        acc_ref[...] = x_ref[...]

    acc_ref[...] += _swiglu_chunk(h_ref[...], wg_ref[...], wu_ref[...], wd_ref[...])

    @pl.when(f == pl.num_programs(1) - 1)
    def _():
        o_ref[...] = acc_ref[...]


def _dense_ffn(h, w_gate_up, w_down, x2d, *, ff_tile):
    rows, d_model = x2d.shape
    d_ff = w_down.shape[0]
    nf = d_ff // ff_tile
    tm = FFN_ROW_TILE
    return pl.pallas_call(
        _dense_ffn_kernel,
        out_shape=jax.ShapeDtypeStruct((rows, d_model), F32),
        grid=(rows // tm, nf),
        in_specs=[pl.BlockSpec((tm, d_model), lambda i, f: (i, 0)),
                  pl.BlockSpec((d_model, ff_tile), lambda i, f: (0, f)),
                  pl.BlockSpec((d_model, ff_tile), lambda i, f: (0, f + nf)),
                  pl.BlockSpec((ff_tile, d_model), lambda i, f: (f, 0)),
                  pl.BlockSpec((tm, d_model), lambda i, f: (i, 0))],
        out_specs=pl.BlockSpec((tm, d_model), lambda i, f: (i, 0)),
        scratch_shapes=[pltpu.VMEM((tm, d_model), F32)],
        compiler_params=_params("parallel", "arbitrary"),
        name="dense_ffn",
    )(h, w_gate_up, w_gate_up, w_down, x2d)


def _route_plan(route, tm):
    n_tokens = route.shape[0]
    expert = jnp.concatenate([route[:, 0], route[:, 1]]).astype(jnp.int32)
    onehot = (expert[:, None] == jnp.arange(N_EXPERTS, dtype=jnp.int32)[None, :]).astype(jnp.int32)
    csum = jnp.cumsum(onehot, axis=0)
    tiles = (csum[-1] + tm - 1) // tm
    tile_end = jnp.cumsum(tiles)
    start = (tile_end - tiles) * tm
    pos = jnp.sum(onehot * (start[None, :] + csum - 1), axis=1)
    n_tiles = 2 * n_tokens // tm + N_EXPERTS
    tile_expert = jnp.sum(jnp.arange(n_tiles, dtype=jnp.int32)[:, None] >= tile_end[None, :], axis=1)
    n_used = tile_end[-1]
    tile_expert = jnp.minimum(tile_expert, tile_expert[n_used - 1])
    pos = pos.astype(jnp.int32)
    token = jnp.arange(2 * n_tokens, dtype=jnp.int32) % n_tokens
    src = jnp.zeros((n_tiles * tm,), jnp.int32).at[pos].set(token, unique_indices=True)
    return pos, src, jnp.concatenate([tile_expert, n_used[None]]).astype(jnp.int32)


def _expert_ffn_kernel(plan_ref, src_ref, h_ref, wg_ref, wu_ref, wd_ref, ys_ref,
                       hs_ref, hb_ref, acc_ref, sem):
    i, f = pl.program_id(0), pl.program_id(1)
    last = pl.num_programs(1) - 1
    used = i < plan_ref[pl.num_programs(0)]
    tm = hs_ref.shape[0]

    def row_copy(r):
        return pltpu.make_async_copy(h_ref.at[pl.ds(src_ref[0, 0, r], 1)],
                                     hs_ref.at[pl.ds(r, 1)], sem)

    def start(r, c):
        row_copy(r).start()
        return c

    def wait(r, c):
        row_copy(r).wait()
        return c

    @pl.when(used & (f == 0))
    def _():
        lax.fori_loop(0, tm, start, 0, unroll=DMA_UNROLL)
        lax.fori_loop(0, tm, wait, 0, unroll=DMA_UNROLL)
        hb_ref[...] = hs_ref[...].astype(BF16)
        acc_ref[...] = jnp.zeros_like(acc_ref)

    @pl.when(used)
    def _():
        acc_ref[...] += _swiglu_chunk(hb_ref[...], wg_ref[0], wu_ref[0], wd_ref[0])

    @pl.when(used & (f == last))
    def _():
        ys_ref[...] = acc_ref[...]

    @pl.when(jnp.logical_not(used) & (f == last))
    def _():
        ys_ref[...] = jnp.zeros_like(ys_ref)


def _expert_ffn(h, src, plan, w_gate_up, w_down, *, ff_tile):
    d_model = h.shape[1]
    d_ff = w_down.shape[1]
    nf = d_ff // ff_tile
    tm = MOE_ROW_TILE
    n_tiles = plan.shape[0] - 1
    n_rows = n_tiles * tm

    def chunk_of(i, f, plan_ref):
        return jnp.where(i < plan_ref[n_tiles], f, nf - 1)

    grid_spec = pltpu.PrefetchScalarGridSpec(
        num_scalar_prefetch=1,
        grid=(n_tiles, nf),
        in_specs=[
            pl.BlockSpec((1, 1, tm), lambda i, f, p: (i, 0, 0), memory_space=pltpu.SMEM),
            pl.BlockSpec(memory_space=pl.ANY),
            pl.BlockSpec((1, d_model, ff_tile), lambda i, f, p: (p[i], 0, chunk_of(i, f, p))),
            pl.BlockSpec((1, d_model, ff_tile), lambda i, f, p: (p[i], 0, chunk_of(i, f, p) + nf)),
            pl.BlockSpec((1, ff_tile, d_model), lambda i, f, p: (p[i], chunk_of(i, f, p), 0)),
        ],
        out_specs=pl.BlockSpec((tm, d_model), lambda i, f, p: (i, 0)),
        scratch_shapes=[pltpu.VMEM((tm, d_model), F32), pltpu.VMEM((tm, d_model), BF16),
                        pltpu.VMEM((tm, d_model), F32), pltpu.SemaphoreType.DMA],
    )
    return pl.pallas_call(
        _expert_ffn_kernel,
        out_shape=jax.ShapeDtypeStruct((n_rows, d_model), F32),
        grid_spec=grid_spec,
        compiler_params=_params("arbitrary", "arbitrary"),
        name="moe_experts",
    )(plan, src.reshape(n_tiles, 1, tm), h, w_gate_up, w_gate_up, w_down)


def _combine_kernel(p1_ref, p2_ref, ys_ref, route_ref, x_ref, o_ref, buf_ref, sem):
    chunk = x_ref.shape[0]

    def row_copy(k, pos_ref, r):
        return pltpu.make_async_copy(ys_ref.at[pl.ds(pos_ref[0, 0, r], 1)],
                                     buf_ref.at[k, pl.ds(r, 1)], sem.at[k])

    def start(r, c):
        row_copy(0, p1_ref, r).start()
        row_copy(1, p2_ref, r).start()
        return c

    def wait(r, c):
        row_copy(0, p1_ref, r).wait()
        row_copy(1, p2_ref, r).wait()
        return c

    lax.fori_loop(0, chunk, start, 0, unroll=DMA_UNROLL)
    lax.fori_loop(0, chunk, wait, 0, unroll=DMA_UNROLL)
    route = route_ref[...]
    o_ref[...] = x_ref[...] + route[:, 2:3] * buf_ref[0] + route[:, 3:4] * buf_ref[1]


def _combine(ys, pos, route, x2d):
    rows, d_model = x2d.shape
    chunk = ROUTE_CHUNK
    n_chunks = rows // chunk
    pos3 = pos.reshape(2 * n_chunks, 1, chunk)
    smem_spec = lambda off: pl.BlockSpec((1, 1, chunk), lambda c: (c + off, 0, 0), memory_space=pltpu.SMEM)
    row_spec = lambda n: pl.BlockSpec((chunk, n), lambda c: (c, 0))
    return pl.pallas_call(
        _combine_kernel,
        out_shape=jax.ShapeDtypeStruct((rows, d_model), F32),
        grid=(n_chunks,),
        in_specs=[smem_spec(0), smem_spec(n_chunks), pl.BlockSpec(memory_space=pl.ANY),
                  row_spec(LANES), row_spec(d_model)],
        out_specs=row_spec(d_model),
        scratch_shapes=[pltpu.VMEM((2, chunk, d_model), F32), pltpu.SemaphoreType.DMA((2,))],
        compiler_params=_params("arbitrary"),
        name="moe_combine",
    )(pos3, pos3, ys, route, x2d)


def _moe_ffn(h, route, w_gate_up, w_down, x2d, *, ff_tile):
    pos, src, plan = _route_plan(route, MOE_ROW_TILE)
    ys = _expert_ffn(h, src, plan, w_gate_up, w_down, ff_tile=ff_tile)
    return _combine(ys, pos, route, x2d)


def _rope_tables(positions):
    half = ROPE_DIM // 2
    inv_freq = ROPE_THETA ** (-jnp.arange(0, ROPE_DIM, 2, dtype=F32) / ROPE_DIM)
    ang = positions.astype(F32).reshape(-1, 1) * inv_freq
    cos, sin = jnp.cos(ang), jnp.sin(ang)
    rows = ang.shape[0]
    pad = jnp.zeros((rows, HEAD_DIM - ROPE_DIM), F32)
    zero = jnp.zeros((rows, half), F32)
    cos_h = jnp.concatenate([cos, cos, pad + 1.0], axis=-1)
    lo_h = jnp.concatenate([-sin, zero, pad], axis=-1)
    hi_h = jnp.concatenate([zero, sin, pad], axis=-1)
    tile = lambda a: jnp.concatenate([a] * (LANES // HEAD_DIM), axis=-1)
    return tile(cos_h), tile(lo_h), tile(hi_h)


def _tile_gain(g, width):
    return jnp.tile(g.astype(F32), width // g.shape[-1]).reshape(1, width)


def _ff_tile(d_ff, limit):
    return max(t for t in range(LANES, limit + 1, LANES) if d_ff % t == 0)


def kernel(x, mem, positions, attn_norm, w_in, w_out, mem_norm, w_mem_kv, mem_q_norm, mem_k_norm,
           diff_q_norm, diff_k_norm, diff_lambda, diff_subln, ffn_norm, dense_w_gate_up,
           dense_w_down, w_router, moe_w_gate_up, moe_w_down):
    b, s, d_model = x.shape
    depth = w_in.shape[0]
    mem_len = mem.shape[1]
    mem_width = w_mem_kv.shape[-1] // 2
    mix_width = (w_in.shape[-1] - mem_width) // 3
    rows = b * s
    assert rows % FFN_ROW_TILE == 0 and s % MEM_Q_TILE == 0 and mix_width % (2 * LANES) == 0

    row = lambda a: a.astype(F32).reshape(1, -1)
    k_gain = jnp.stack([_tile_gain(mem_k_norm[i], mem_width) for i in range(depth)])
    km, vm = _mem_kv(mem.reshape(b * mem_len, d_model), row(mem_norm), w_mem_kv.astype(BF16), k_gain)
    km = km.reshape(depth, b, mem_len, mem_width)
    vm = vm.reshape(depth, b, mem_len, mem_width)
    cos, sin_lo, sin_hi = _rope_tables(positions)

    x2d = x.reshape(rows, d_model)
    for i in range(depth):
        j = i // 2
        is_diff = i % 2 == 1
        diff_args = None
        if is_diff:
            diff_args = (_tile_gain(diff_q_norm[j], LANES), _tile_gain(diff_k_norm[j], LANES),
                         cos, sin_lo, sin_hi)
        q, k, v, qm = _in_proj(x2d, row(attn_norm[i]), w_in[i].astype(BF16),
                               _tile_gain(mem_q_norm[i], LANES), diff_args, mix_width=mix_width)
        to3 = lambda a: a.reshape(b, s, a.shape[-1])
        o_mem = _mem_attn(to3(qm), km[i], vm[i])
        if is_diff:
            lam_init = 0.8 - 0.6 * math.exp(-0.3 * i)
            lp = diff_lambda[j].astype(F32)
            lam = jnp.exp(jnp.sum(lp[0] * lp[1])) - jnp.exp(jnp.sum(lp[2] * lp[3])) + lam_init
            o_mix = _token_attn(functools.partial(_diff_kernel, out_scale=1.0 - lam_init), "diff_attn",
                                DIFF_TILE, to3(q), to3(k), to3(v),
                                extra=(lam.reshape(1, 1), row(diff_subln[j])))
        else:
            o_mix = _token_attn(_sb_kernel, "sb_attn", SB_TILE, to3(q), to3(k), to3(v))
        router = None
        if is_diff:
            router = jnp.pad(w_router[j].astype(F32), ((0, 0), (0, LANES - w_router.shape[-1])))
        outs = _out_proj(o_mix.reshape(rows, mix_width), o_mem.reshape(rows, mem_width),
                         w_out[i].astype(BF16), x2d, row(ffn_norm[i]), router)
        if is_diff:
            x2d, h, route = outs
            x2d = _moe_ffn(h, route, moe_w_gate_up[j].astype(BF16), moe_w_down[j].astype(BF16), x2d,
                           ff_tile=_ff_tile(moe_w_down.shape[-2], 1024))
        else:
            x2d, h = outs
            x2d = _dense_ffn(h, dense_w_gate_up[j].astype(BF16), dense_w_down[j].astype(BF16), x2d,
                             ff_tile=_ff_tile(dense_w_down.shape[-2], 1536))
    return x2d.reshape(b, s, d_model)
```

```python
import functools
import math

import jax
import jax.numpy as jnp
from jax import lax
from jax.experimental import pallas as pl
from jax.experimental.pallas import tpu as pltpu

F32 = jnp.float32
BF16 = jnp.bfloat16

HEAD_DIM = 64
LANES = 128
N_MEM_HEADS = 4
ROPE_DIM = HEAD_DIM // 4
ROPE_THETA = 500000.0
N_EXPERTS = 8
EPS = 1e-6
NEG_BIG = -1e30
SCALE = HEAD_DIM ** -0.5
LOG2E = math.log2(math.e)
SB_DONE = 104.0
VMEM_LIMIT = 48 * 1024 * 1024

ROW_TILE = 512
FFN_ROW_TILE = 1024
SB_TILE = 256
DIFF_TILE = 512
MOE_ROW_TILE = 1024
ROUTE_CHUNK = 512
DMA_UNROLL = 8
MEM_Q_TILE = 1024


def _params(*sem):
    return pltpu.CompilerParams(dimension_semantics=sem, vmem_limit_bytes=VMEM_LIMIT)


def _split_bf16(x):
    hi = x.astype(BF16)
    lo = (x - hi.astype(F32)).astype(BF16)
    return hi, lo


def _dot(a, b):
    return jnp.dot(a, b, preferred_element_type=F32)


def _dot_nt(a, b):
    return lax.dot_general(a, b, (((1,), (1,)), ((), ())), preferred_element_type=F32)


def _dot_split(x, m):
    hi, lo = _split_bf16(x)
    return _dot(hi, m) + _dot(lo, m)


def _group_ones(n, group):
    r = lax.broadcasted_iota(jnp.int32, (n, n), 0) // group
    c = lax.broadcasted_iota(jnp.int32, (n, n), 1) // group
    return (r == c).astype(BF16)


def _head_rms(t, gain, ones):
    ss = _dot_split(t * t, ones)
    return t * lax.rsqrt(ss * (1.0 / HEAD_DIM) + EPS) * gain


def _rms(x, g):
    return x * lax.rsqrt(jnp.mean(x * x, axis=-1, keepdims=True) + EPS) * g


def _mem_kv_kernel(mem_ref, g_ref, w_ref, kg_ref, k_out, v_out):
    width = k_out.shape[-1]
    mem_n = _rms(mem_ref[...], g_ref[...]).astype(BF16)
    kv = _dot(mem_n, w_ref[0])
    ones = _group_ones(width, HEAD_DIM)
    k_out[0] = _head_rms(kv[:, :width], kg_ref[0], ones).astype(BF16)
    v_out[0] = kv[:, width:].astype(BF16)


def _mem_kv(mem2d, mem_norm, w_mem_kv, k_gain):
    depth, d_model, two_w = w_mem_kv.shape
    width = two_w // 2
    rows = mem2d.shape[0]
    out = jax.ShapeDtypeStruct((depth, rows, width), BF16)
    return pl.pallas_call(
        _mem_kv_kernel,
        out_shape=(out, out),
        grid=(depth,),
        in_specs=[
            pl.BlockSpec((rows, d_model), lambda i: (0, 0)),
            pl.BlockSpec((1, d_model), lambda i: (0, 0)),
            pl.BlockSpec((1, d_model, two_w), lambda i: (i, 0, 0)),
            pl.BlockSpec((1, 1, width), lambda i: (i, 0, 0)),
        ],
        out_specs=(pl.BlockSpec((1, rows, width), lambda i: (i, 0, 0)),
                   pl.BlockSpec((1, rows, width), lambda i: (i, 0, 0))),
        compiler_params=_params("arbitrary"),
        name="mem_kv",
    )(mem2d, mem_norm, w_mem_kv, k_gain)


def _rope(t, cos, sin_lo, sin_hi):
    n = t.shape[-1]
    half = ROPE_DIM // 2
    return t * cos + pltpu.roll(t, n - half, 1) * sin_lo + pltpu.roll(t, half, 1) * sin_hi


def _in_proj_kernel(*refs, mix_width, diff):
    if diff:
        (x_ref, g_ref, w_ref, mg_ref, qg_ref, kg_ref, cos_ref, slo_ref, shi_ref,
         q_out, k_out, v_out, m_out) = refs
    else:
        x_ref, g_ref, w_ref, mg_ref, q_out, k_out, v_out, m_out = refs
    h = _rms(x_ref[...], g_ref[...]).astype(BF16)
    ones = _group_ones(LANES, HEAD_DIM)
    chunk = 2 * LANES
    for c in range(mix_width // chunk):
        lo = c * chunk
        q = _dot(h, w_ref[:, lo:lo + chunk])
        k = _dot(h, w_ref[:, mix_width + lo:mix_width + lo + chunk])
        for half in range(2):
            sl = slice(half * LANES, (half + 1) * LANES)
            dst = slice(lo + half * LANES, lo + (half + 1) * LANES)
            qh, kh = q[:, sl], k[:, sl]
            if diff:
                rope = functools.partial(_rope, cos=cos_ref[...], sin_lo=slo_ref[...],
                                         sin_hi=shi_ref[...])
                qh = rope(_head_rms(qh, qg_ref[...], ones))
                kh = rope(_head_rms(kh, kg_ref[...], ones))
            q_out[:, dst] = (qh * (SCALE * LOG2E if diff else SCALE)).astype(BF16)
            k_out[:, dst] = kh.astype(BF16)
    v_out[...] = _dot(h, w_ref[:, 2 * mix_width:3 * mix_width]).astype(BF16)
    qm = _dot(h, w_ref[:, 3 * mix_width:])
    mem_width = qm.shape[-1]
    for c in range(mem_width // LANES):
        sl = slice(c * LANES, (c + 1) * LANES)
        m_out[:, sl] = (_head_rms(qm[:, sl], mg_ref[...], ones) * SCALE).astype(BF16)


def _in_proj(x2d, g, w, mem_q_gain, diff_args, *, mix_width):
    rows, d_model = x2d.shape
    in_width = w.shape[1]
    mem_width = in_width - 3 * mix_width
    tm = ROW_TILE
    row_spec = lambda n: pl.BlockSpec((tm, n), lambda i: (i, 0))
    const_spec = lambda a: pl.BlockSpec(a.shape, lambda i: (0, 0))
    diff = diff_args is not None
    in_specs = [row_spec(d_model), const_spec(g), const_spec(w), const_spec(mem_q_gain)]
    args = [x2d, g, w, mem_q_gain]
    if diff:
        q_gain, k_gain, cos, sin_lo, sin_hi = diff_args
        in_specs += [const_spec(q_gain), const_spec(k_gain)] + [row_spec(LANES)] * 3
        args += [q_gain, k_gain, cos, sin_lo, sin_hi]
    mix = jax.ShapeDtypeStruct((rows, mix_width), BF16)
    return pl.pallas_call(
        functools.partial(_in_proj_kernel, mix_width=mix_width, diff=diff),
        out_shape=(mix, mix, mix, jax.ShapeDtypeStruct((rows, mem_width), BF16)),
        grid=(rows // tm,),
        in_specs=in_specs,
        out_specs=(row_spec(mix_width), row_spec(mix_width), row_spec(mix_width),
                   row_spec(mem_width)),
        compiler_params=_params("parallel"),
        name="in_proj_diff" if diff else "in_proj_sb",
    )(*args)


def _mem_attn_kernel(q_ref, k_ref, v_ref, o_ref):
    q, k, v = q_ref[0], k_ref[0], v_ref[0]
    lane = lax.broadcasted_iota(jnp.int32, (1, q.shape[-1]), 1) // HEAD_DIM
    out = jnp.zeros(q.shape, F32)
    for hd in range(N_MEM_HEADS):
        sel = lane == hd
        s = _dot_nt(jnp.where(sel, q, 0), k)
        p = jnp.exp(s - jnp.max(s, axis=-1, keepdims=True))
        p = p / jnp.sum(p, axis=-1, keepdims=True)
        out = out + _dot(p.astype(BF16), jnp.where(sel, v, 0))
    o_ref[0] = out.astype(BF16)


def _mem_attn(qm, km, vm):
    b, s, width = qm.shape
    m = km.shape[1]
    tq = MEM_Q_TILE
    return pl.pallas_call(
        _mem_attn_kernel,
        out_shape=jax.ShapeDtypeStruct((b, s, width), BF16),
        grid=(b, s // tq),
        in_specs=[pl.BlockSpec((1, tq, width), lambda bi, i: (bi, i, 0)),
                  pl.BlockSpec((1, m, width), lambda bi, i: (bi, 0, 0)),
                  pl.BlockSpec((1, m, width), lambda bi, i: (bi, 0, 0))],
        out_specs=pl.BlockSpec((1, tq, width), lambda bi, i: (bi, i, 0)),
        compiler_params=_params("parallel", "parallel"),
        name="mem_attn",
    )(qm, km, vm)


def _sb_kernel(q_ref, k_ref, v_ref, o_ref):
    t = q_ref.shape[1]
    i = pl.program_id(2)
    q = q_ref[0]
    lane = lax.broadcasted_iota(jnp.int32, (1, LANES), 1)
    first = lane < HEAD_DIM
    q_heads = (jnp.where(first, q, 0), jnp.where(first, 0, q))
    row = lax.broadcasted_iota(jnp.int32, (t, t), 0)
    col = lax.broadcasted_iota(jnp.int32, (t, t), 1)
    later = (row > col).astype(BF16)
    strict = col < row

    def block(j, carry, mask):
        spent_a, spent_b, o = carry
        kb = k_ref[0, pl.ds(j * t, t), :]
        vb = v_ref[0, pl.ds(j * t, t), :]
        v_heads = (jnp.where(first, vb, 0), jnp.where(first, 0, vb))
        spent = [spent_a, spent_b]
        for hd in range(2):
            z = _dot_nt(q_heads[hd], kb)
            sp = jnp.maximum(z, 0.0) + jnp.log(1.0 + jnp.exp(-jnp.abs(z)))
            if mask is not None:
                sp = jnp.where(mask, sp, 0.0)
            after = _dot_split(sp, later) + spent[hd]
            w = jnp.exp(z - sp - after)
            if mask is not None:
                w = jnp.where(mask, w, 0.0)
            o = o + _dot(w.astype(BF16), v_heads[hd])
            spent[hd] = spent[hd] + jnp.sum(sp, axis=-1, keepdims=True)
        return spent[0], spent[1], o

    def stick_left(spent_a, spent_b):
        return (jnp.min(jnp.minimum(spent_a, spent_b)) < SB_DONE).astype(jnp.int32)

    def earlier(state):
        n, _, carry = state
        carry = block(i - 1 - n, carry, None)
        return n + 1, stick_left(carry[0], carry[1]), carry

    zero = jnp.zeros((t, 1), F32)
    carry = block(i, (zero, zero, jnp.zeros((t, LANES), F32)), strict)
    state = (jnp.int32(0), stick_left(carry[0], carry[1]), carry)
    state = lax.while_loop(lambda st: (st[0] < i) & (st[1] > 0), earlier, state)
    o_ref[0] = state[2][2].astype(BF16)


def _token_attn(kernel, name, t, q, k, v, extra=(), scratch=()):
    b, s, width = q.shape
    q_spec = pl.BlockSpec((1, t, LANES), lambda bi, p, i: (bi, i, p))
    kv_spec = pl.BlockSpec((1, s, LANES), lambda bi, p, i: (bi, 0, p))
    extra_specs = [pl.BlockSpec(a.shape, lambda bi, p, i: (0, 0)) for a in extra]
    return pl.pallas_call(
        kernel,
        out_shape=jax.ShapeDtypeStruct((b, s, width), BF16),
        grid=(b, width // LANES, s // t),
        in_specs=[q_spec, kv_spec, kv_spec] + extra_specs,
        out_specs=q_spec,
        scratch_shapes=list(scratch),
        compiler_params=_params("parallel", "parallel", "arbitrary"),
        name=name,
    )(q, k, v, *extra)


def _diff_kernel(q_ref, k_ref, v_ref, lam_ref, g_ref, o_ref, *s_refs, out_scale):
    t = q_ref.shape[1]
    i = pl.program_id(2)
    q = q_ref[0]
    lane = lax.broadcasted_iota(jnp.int32, (1, LANES), 1)
    first = lane < HEAD_DIM
    q_maps = (jnp.where(first, q, 0), jnp.where(first, 0, q))
    key = lax.broadcasted_iota(jnp.int32, (t, t), 0)
    qry = lax.broadcasted_iota(jnp.int32, (t, t), 1)
    causal = key <= qry

    def scores(j, slot, mask):
        kb = k_ref[0, pl.ds(j * t, t), :]
        tops = []
        for c in range(2):
            s = _dot_nt(kb, q_maps[c])
            if mask is not None:
                s = jnp.where(mask, s, NEG_BIG)
            s_refs[slot][c] = s
            tops.append(jnp.max(s, axis=0, keepdims=True))
        return tuple(tops)

    def absorb(j, slot, tops, carry):
        vb = v_ref[0, pl.ds(j * t, t), :]
        new = []
        for c in range(2):
            m, l, acc = carry[c]
            m_new = jnp.maximum(m, tops[c])
            alpha = jnp.exp2(m - m_new)
            p = jnp.exp2(s_refs[slot][c] - m_new)
            l = alpha * l + jnp.sum(p, axis=0, keepdims=True)
            pv = lax.dot_general(vb, p.astype(BF16), (((0,), (0,)), ((), ())),
                                 preferred_element_type=F32)
            new.append((m_new, l, alpha * acc + pv))
        return tuple(new)

    def step(n, slot, state):
        tops, carry = state
        tops_next = scores(n, 1 - slot, None)
        return tops_next, absorb(jnp.where(n == 0, i, n - 1), slot, tops, carry)

    def pair(n2, state):
        return step(2 * n2 + 1, 1, step(2 * n2, 0, state))

    def tail_even(state):
        tops, carry = state
        return absorb(jnp.maximum(i - 1, 0), 0, tops, carry)

    def tail_odd(state):
        tops, carry = step(i - 1, 0, state)
        return absorb(i - 1, 1, tops, carry)

    init = (jnp.full((1, t), NEG_BIG, F32), jnp.zeros((1, t), F32), jnp.zeros((LANES, t), F32))
    state = lax.fori_loop(0, i // 2, pair, (scores(i, 0, causal), (init, init)))
    carry = lax.cond(i % 2 == 1, tail_odd, tail_even, state)
    (_, l1, acc1), (_, l2, acc2) = carry
    o = acc1 / l1 - lam_ref[...] * (acc2 / l2)
    o = o * lax.rsqrt(jnp.mean(o * o, axis=0, keepdims=True) + EPS)
    o_ref[0] = (o.T * (g_ref[...] * out_scale)).astype(BF16)


def _top2_route(logits):
    lane = lax.broadcasted_iota(jnp.int32, logits.shape, 1)
    lg = jnp.where(lane < N_EXPERTS, logits, -jnp.inf)
    m1 = jnp.max(lg, axis=-1, keepdims=True)
    i1 = jnp.min(jnp.where(lg == m1, lane, LANES), axis=-1, keepdims=True)
    lg2 = jnp.where(lane == i1, -jnp.inf, lg)
    m2 = jnp.max(lg2, axis=-1, keepdims=True)
    i2 = jnp.min(jnp.where(lg2 == m2, lane, LANES), axis=-1, keepdims=True)
    e = jnp.exp(m2 - m1)
    w1 = 1.0 / (1.0 + e)
    fields = (i1.astype(F32), i2.astype(F32), w1, e * w1)
    out = jnp.zeros(logits.shape, F32)
    for n, val in enumerate(fields):
        out = jnp.where(lane == n, val, out)
    return out


def _out_proj_kernel(*refs, moe):
    if moe:
        mix_ref, mem_ref, w_ref, x_ref, g_ref, wr_ref, x_out, h_out, c_out = refs
    else:
        mix_ref, mem_ref, w_ref, x_ref, g_ref, x_out, h_out = refs
    mix_width = mix_ref.shape[-1]
    x = x_ref[...] + _dot(mix_ref[...], w_ref[:mix_width, :]) + _dot(mem_ref[...], w_ref[mix_width:, :])
    x_out[...] = x
    h = _rms(x, g_ref[...])
    h_out[...] = h.astype(h_out.dtype)
    if moe:
        h_hi, h_lo = _split_bf16(h)
        w_hi, w_lo = _split_bf16(wr_ref[...])
        c_out[...] = _top2_route(_dot(h_hi, w_hi) + _dot(h_hi, w_lo) + _dot(h_lo, w_hi))


def _out_proj(o_mix, o_mem, w, x2d, g, w_router):
    rows, d_model = x2d.shape
    tm = ROW_TILE
    row_spec = lambda n: pl.BlockSpec((tm, n), lambda i: (i, 0))
    const_spec = lambda a: pl.BlockSpec(a.shape, lambda i: (0, 0))
    moe = w_router is not None
    in_specs = [row_spec(o_mix.shape[1]), row_spec(o_mem.shape[1]), const_spec(w),
                row_spec(d_model), const_spec(g)]
    args = [o_mix, o_mem, w, x2d, g]
    out_shape = [jax.ShapeDtypeStruct((rows, d_model), F32),
                 jax.ShapeDtypeStruct((rows, d_model), F32 if moe else BF16)]
    out_specs = [row_spec(d_model), row_spec(d_model)]
    if moe:
        in_specs.append(const_spec(w_router))
        args.append(w_router)
        out_shape.append(jax.ShapeDtypeStruct((rows, LANES), F32))
        out_specs.append(row_spec(LANES))
    return pl.pallas_call(
        functools.partial(_out_proj_kernel, moe=moe),
        out_shape=tuple(out_shape),
        grid=(rows // tm,),
        in_specs=in_specs,
        out_specs=tuple(out_specs),
        compiler_params=_params("parallel"),
        name="out_proj_moe" if moe else "out_proj",
    )(*args)


def _swiglu_chunk(h, wg, wu, wd):
    g = _dot(h, wg)
    u = _dot(h, wu)
    return _dot((g * jax.nn.sigmoid(g) * u).astype(BF16), wd)


def _dense_ffn_kernel(h_ref, wg_ref, wu_ref, wd_ref, x_ref, o_ref, acc_ref):
    f = pl.program_id(1)

    @pl.when(f == 0)
    def _():
        acc_ref[...] = x_ref[...]

    acc_ref[...] += _swiglu_chunk(h_ref[...], wg_ref[...], wu_ref[...], wd_ref[...])

    @pl.when(f == pl.num_programs(1) - 1)
    def _():
        o_ref[...] = acc_ref[...]


def _dense_ffn(h, w_gate_up, w_down, x2d, *, ff_tile):
    rows, d_model = x2d.shape
    d_ff = w_down.shape[0]
    nf = d_ff // ff_tile
    tm = FFN_ROW_TILE
    return pl.pallas_call(
        _dense_ffn_kernel,
        out_shape=jax.ShapeDtypeStruct((rows, d_model), F32),
        grid=(rows // tm, nf),
        in_specs=[pl.BlockSpec((tm, d_model), lambda i, f: (i, 0)),
                  pl.BlockSpec((d_model, ff_tile), lambda i, f: (0, f)),
                  pl.BlockSpec((d_model, ff_tile), lambda i, f: (0, f + nf)),
                  pl.BlockSpec((ff_tile, d_model), lambda i, f: (f, 0)),
                  pl.BlockSpec((tm, d_model), lambda i, f: (i, 0))],
        out_specs=pl.BlockSpec((tm, d_model), lambda i, f: (i, 0)),
        scratch_shapes=[pltpu.VMEM((tm, d_model), F32)],
        compiler_params=_params("parallel", "arbitrary"),
        name="dense_ffn",
    )(h, w_gate_up, w_gate_up, w_down, x2d)


def _route_plan(route, tm):
    n_tokens = route.shape[0]
    expert = jnp.concatenate([route[:, 0], route[:, 1]]).astype(jnp.int32)
    onehot = (expert[:, None] == jnp.arange(N_EXPERTS, dtype=jnp.int32)[None, :]).astype(jnp.int32)
    csum = jnp.cumsum(onehot, axis=0)
    tiles = (csum[-1] + tm - 1) // tm
    tile_end = jnp.cumsum(tiles)
    start = (tile_end - tiles) * tm
    pos = jnp.sum(onehot * (start[None, :] + csum - 1), axis=1)
    n_tiles = 2 * n_tokens // tm + N_EXPERTS
    tile_expert = jnp.sum(jnp.arange(n_tiles, dtype=jnp.int32)[:, None] >= tile_end[None, :], axis=1)
    n_used = tile_end[-1]
    tile_expert = jnp.minimum(tile_expert, tile_expert[n_used - 1])
    pos = pos.astype(jnp.int32)
    token = jnp.arange(2 * n_tokens, dtype=jnp.int32) % n_tokens
    src = jnp.zeros((n_tiles * tm,), jnp.int32).at[pos].set(token, unique_indices=True)
    return pos, src, jnp.concatenate([tile_expert, n_used[None]]).astype(jnp.int32)


def _expert_ffn_kernel(plan_ref, src_ref, h_ref, wg_ref, wu_ref, wd_ref, ys_ref,
                       hs_ref, hb_ref, acc_ref, sem):
    i, f = pl.program_id(0), pl.program_id(1)
    last = pl.num_programs(1) - 1
    used = i < plan_ref[pl.num_programs(0)]
    tm = hs_ref.shape[0]

    def row_copy(r):
        return pltpu.make_async_copy(h_ref.at[pl.ds(src_ref[0, 0, r], 1)],
                                     hs_ref.at[pl.ds(r, 1)], sem)

    def start(r, c):
        row_copy(r).start()
        return c

    def wait(r, c):
        row_copy(r).wait()
        return c

    @pl.when(used & (f == 0))
    def _():
        lax.fori_loop(0, tm, start, 0, unroll=DMA_UNROLL)
        lax.fori_loop(0, tm, wait, 0, unroll=DMA_UNROLL)
        hb_ref[...] = hs_ref[...].astype(BF16)
        acc_ref[...] = jnp.zeros_like(acc_ref)

    @pl.when(used)
    def _():
        acc_ref[...] += _swiglu_chunk(hb_ref[...], wg_ref[0], wu_ref[0], wd_ref[0])

    @pl.when(used & (f == last))
    def _():
        ys_ref[...] = acc_ref[...]

    @pl.when(jnp.logical_not(used) & (f == last))
    def _():
        ys_ref[...] = jnp.zeros_like(ys_ref)


def _expert_ffn(h, src, plan, w_gate_up, w_down, *, ff_tile):
    d_model = h.shape[1]
    d_ff = w_down.shape[1]
    nf = d_ff // ff_tile
    tm = MOE_ROW_TILE
    n_tiles = plan.shape[0] - 1
    n_rows = n_tiles * tm

    def chunk_of(i, f, plan_ref):
        return jnp.where(i < plan_ref[n_tiles], f, nf - 1)

    grid_spec = pltpu.PrefetchScalarGridSpec(
        num_scalar_prefetch=1,
        grid=(n_tiles, nf),
        in_specs=[
            pl.BlockSpec((1, 1, tm), lambda i, f, p: (i, 0, 0), memory_space=pltpu.SMEM),
            pl.BlockSpec(memory_space=pl.ANY),
            pl.BlockSpec((1, d_model, ff_tile), lambda i, f, p: (p[i], 0, chunk_of(i, f, p))),
            pl.BlockSpec((1, d_model, ff_tile), lambda i, f, p: (p[i], 0, chunk_of(i, f, p) + nf)),
            pl.BlockSpec((1, ff_tile, d_model), lambda i, f, p: (p[i], chunk_of(i, f, p), 0)),
        ],
        out_specs=pl.BlockSpec((tm, d_model), lambda i, f, p: (i, 0)),
        scratch_shapes=[pltpu.VMEM((tm, d_model), F32), pltpu.VMEM((tm, d_model), BF16),
                        pltpu.VMEM((tm, d_model), F32), pltpu.SemaphoreType.DMA],
    )
    return pl.pallas_call(
        _expert_ffn_kernel,
        out_shape=jax.ShapeDtypeStruct((n_rows, d_model), F32),
        grid_spec=grid_spec,
        compiler_params=_params("arbitrary", "arbitrary"),
        name="moe_experts",
    )(plan, src.reshape(n_tiles, 1, tm), h, w_gate_up, w_gate_up, w_down)


def _combine_kernel(p1_ref, p2_ref, ys_ref, route_ref, x_ref, o_ref, buf_ref, sem):
    chunk = x_ref.shape[0]

    def row_copy(k, pos_ref, r):
        return pltpu.make_async_copy(ys_ref.at[pl.ds(pos_ref[0, 0, r], 1)],
                                     buf_ref.at[k, pl.ds(r, 1)], sem.at[k])

    def start(r, c):
        row_copy(0, p1_ref, r).start()
        row_copy(1, p2_ref, r).start()
        return c

    def wait(r, c):
        row_copy(0, p1_ref, r).wait()
        row_copy(1, p2_ref, r).wait()
        return c

    lax.fori_loop(0, chunk, start, 0, unroll=DMA_UNROLL)
    lax.fori_loop(0, chunk, wait, 0, unroll=DMA_UNROLL)
    route = route_ref[...]
    o_ref[...] = x_ref[...] + route[:, 2:3] * buf_ref[0] + route[:, 3:4] * buf_ref[1]


def _combine(ys, pos, route, x2d):
    rows, d_model = x2d.shape
    chunk = ROUTE_CHUNK
    n_chunks = rows // chunk
    pos3 = pos.reshape(2 * n_chunks, 1, chunk)
    smem_spec = lambda off: pl.BlockSpec((1, 1, chunk), lambda c: (c + off, 0, 0), memory_space=pltpu.SMEM)
    row_spec = lambda n: pl.BlockSpec((chunk, n), lambda c: (c, 0))
    return pl.pallas_call(
        _combine_kernel,
        out_shape=jax.ShapeDtypeStruct((rows, d_model), F32),
        grid=(n_chunks,),
        in_specs=[smem_spec(0), smem_spec(n_chunks), pl.BlockSpec(memory_space=pl.ANY),
                  row_spec(LANES), row_spec(d_model)],
        out_specs=row_spec(d_model),
        scratch_shapes=[pltpu.VMEM((2, chunk, d_model), F32), pltpu.SemaphoreType.DMA((2,))],
        compiler_params=_params("arbitrary"),
        name="moe_combine",
    )(pos3, pos3, ys, route, x2d)


def _moe_ffn(h, route, w_gate_up, w_down, x2d, *, ff_tile):
    pos, src, plan = _route_plan(route, MOE_ROW_TILE)
    ys = _expert_ffn(h, src, plan, w_gate_up, w_down, ff_tile=ff_tile)
    return _combine(ys, pos, route, x2d)


def _rope_tables(positions):
    half = ROPE_DIM // 2
    inv_freq = ROPE_THETA ** (-jnp.arange(0, ROPE_DIM, 2, dtype=F32) / ROPE_DIM)
    ang = positions.astype(F32).reshape(-1, 1) * inv_freq
    cos, sin = jnp.cos(ang), jnp.sin(ang)
    rows = ang.shape[0]
    pad = jnp.zeros((rows, HEAD_DIM - ROPE_DIM), F32)
    zero = jnp.zeros((rows, half), F32)
    cos_h = jnp.concatenate([cos, cos, pad + 1.0], axis=-1)
    lo_h = jnp.concatenate([-sin, zero, pad], axis=-1)
    hi_h = jnp.concatenate([zero, sin, pad], axis=-1)
    tile = lambda a: jnp.concatenate([a] * (LANES // HEAD_DIM), axis=-1)
    return tile(cos_h), tile(lo_h), tile(hi_h)


def _tile_gain(g, width):
    return jnp.tile(g.astype(F32), width // g.shape[-1]).reshape(1, width)


def _ff_tile(d_ff, limit):
    return max(t for t in range(LANES, limit + 1, LANES) if d_ff % t == 0)


def kernel(x, mem, positions, attn_norm, w_in, w_out, mem_norm, w_mem_kv, mem_q_norm, mem_k_norm,
           diff_q_norm, diff_k_norm, diff_lambda, diff_subln, ffn_norm, dense_w_gate_up,
           dense_w_down, w_router, moe_w_gate_up, moe_w_down):
    b, s, d_model = x.shape
    depth = w_in.shape[0]
    mem_len = mem.shape[1]
    mem_width = w_mem_kv.shape[-1] // 2
    mix_width = (w_in.shape[-1] - mem_width) // 3
    rows = b * s
    assert rows % FFN_ROW_TILE == 0 and s % MEM_Q_TILE == 0 and mix_width % (2 * LANES) == 0

    row = lambda a: a.astype(F32).reshape(1, -1)
    k_gain = jnp.stack([_tile_gain(mem_k_norm[i], mem_width) for i in range(depth)])
    km, vm = _mem_kv(mem.reshape(b * mem_len, d_model), row(mem_norm), w_mem_kv.astype(BF16), k_gain)
    km = km.reshape(depth, b, mem_len, mem_width)
    vm = vm.reshape(depth, b, mem_len, mem_width)
    cos, sin_lo, sin_hi = _rope_tables(positions)

    x2d = x.reshape(rows, d_model)
    for i in range(depth):
        j = i // 2
        is_diff = i % 2 == 1
        diff_args = None
        if is_diff:
            diff_args = (_tile_gain(diff_q_norm[j], LANES), _tile_gain(diff_k_norm[j], LANES),
                         cos, sin_lo, sin_hi)
        q, k, v, qm = _in_proj(x2d, row(attn_norm[i]), w_in[i].astype(BF16),
                               _tile_gain(mem_q_norm[i], LANES), diff_args, mix_width=mix_width)
        to3 = lambda a: a.reshape(b, s, a.shape[-1])
        o_mem = _mem_attn(to3(qm), km[i], vm[i])
        if is_diff:
            lam_init = 0.8 - 0.6 * math.exp(-0.3 * i)
            lp = diff_lambda[j].astype(F32)
            lam = jnp.exp(jnp.sum(lp[0] * lp[1])) - jnp.exp(jnp.sum(lp[2] * lp[3])) + lam_init
            o_mix = _token_attn(functools.partial(_diff_kernel, out_scale=1.0 - lam_init), "diff_attn",
                                DIFF_TILE, to3(q), to3(k), to3(v),
                                extra=(lam.reshape(1, 1), row(diff_subln[j])),
                                scratch=(pltpu.VMEM((2, DIFF_TILE, DIFF_TILE), F32),) * 2)
        else:
            o_mix = _token_attn(_sb_kernel, "sb_attn", SB_TILE, to3(q), to3(k), to3(v))
        router = None
        if is_diff:
            router = jnp.pad(w_router[j].astype(F32), ((0, 0), (0, LANES - w_router.shape[-1])))
        outs = _out_proj(o_mix.reshape(rows, mix_width), o_mem.reshape(rows, mem_width),
                         w_out[i].astype(BF16), x2d, row(ffn_norm[i]), router)
        if is_diff:
            x2d, h, route = outs
            x2d = _moe_ffn(h, route, moe_w_gate_up[j].astype(BF16), moe_w_down[j].astype(BF16), x2d,
                           ff_tile=_ff_tile(moe_w_down.shape[-2], 1024))
        else:
            x2d, h = outs
            x2d = _dense_ffn(h, dense_w_gate_up[j].astype(BF16), dense_w_down[j].astype(BF16), x2d,
                             ff_tile=_ff_tile(dense_w_down.shape[-2], 1536))
    return x2d.reshape(b, s, d_model)
```

```python
import functools
import math

import jax
import jax.numpy as jnp
from jax import lax
from jax.experimental import pallas as pl
from jax.experimental.pallas import tpu as pltpu

F32 = jnp.float32
BF16 = jnp.bfloat16

HEAD_DIM = 64
LANES = 128
N_MEM_HEADS = 4
ROPE_DIM = HEAD_DIM // 4
ROPE_THETA = 500000.0
N_EXPERTS = 8
EPS = 1e-6
NEG_BIG = -1e30
SCALE = HEAD_DIM ** -0.5
LOG2E = math.log2(math.e)
SB_DONE = 104.0 * LOG2E
VMEM_LIMIT = 48 * 1024 * 1024

ROW_TILE = 512
FFN_ROW_TILE = 1024
SB_TILE = 256
DIFF_TILE = 512
MOE_ROW_TILE = 1024
ROUTE_CHUNK = 512
DMA_UNROLL = 8
MEM_Q_TILE = 1024


def _params(*sem):
    return pltpu.CompilerParams(dimension_semantics=sem, vmem_limit_bytes=VMEM_LIMIT)


def _split_bf16(x):
    hi = x.astype(BF16)
    lo = (x - hi.astype(F32)).astype(BF16)
    return hi, lo


def _dot(a, b):
    return jnp.dot(a, b, preferred_element_type=F32)


def _dot_nt(a, b):
    return lax.dot_general(a, b, (((1,), (1,)), ((), ())), preferred_element_type=F32)


def _dot_split(x, m):
    hi, lo = _split_bf16(x)
    return _dot(hi, m) + _dot(lo, m)


def _group_ones(n, group):
    r = lax.broadcasted_iota(jnp.int32, (n, n), 0) // group
    c = lax.broadcasted_iota(jnp.int32, (n, n), 1) // group
    return (r == c).astype(BF16)


def _head_rms(t, gain, ones):
    ss = _dot_split(t * t, ones)
    return t * lax.rsqrt(ss * (1.0 / HEAD_DIM) + EPS) * gain


def _rms(x, g):
    return x * lax.rsqrt(jnp.mean(x * x, axis=-1, keepdims=True) + EPS) * g


def _mem_kv_kernel(mem_ref, g_ref, w_ref, kg_ref, k_out, v_out):
    width = k_out.shape[-1]
    mem_n = _rms(mem_ref[...], g_ref[...]).astype(BF16)
    kv = _dot(mem_n, w_ref[0])
    ones = _group_ones(width, HEAD_DIM)
    k_out[0] = _head_rms(kv[:, :width], kg_ref[0], ones).astype(BF16)
    v_out[0] = kv[:, width:].astype(BF16)


def _mem_kv(mem2d, mem_norm, w_mem_kv, k_gain):
    depth, d_model, two_w = w_mem_kv.shape
    width = two_w // 2
    rows = mem2d.shape[0]
    out = jax.ShapeDtypeStruct((depth, rows, width), BF16)
    return pl.pallas_call(
        _mem_kv_kernel,
        out_shape=(out, out),
        grid=(depth,),
        in_specs=[
            pl.BlockSpec((rows, d_model), lambda i: (0, 0)),
            pl.BlockSpec((1, d_model), lambda i: (0, 0)),
            pl.BlockSpec((1, d_model, two_w), lambda i: (i, 0, 0)),
            pl.BlockSpec((1, 1, width), lambda i: (i, 0, 0)),
        ],
        out_specs=(pl.BlockSpec((1, rows, width), lambda i: (i, 0, 0)),
                   pl.BlockSpec((1, rows, width), lambda i: (i, 0, 0))),
        compiler_params=_params("arbitrary"),
        name="mem_kv",
    )(mem2d, mem_norm, w_mem_kv, k_gain)


def _rope(t, cos, sin_lo, sin_hi):
    n = t.shape[-1]
    half = ROPE_DIM // 2
    return t * cos + pltpu.roll(t, n - half, 1) * sin_lo + pltpu.roll(t, half, 1) * sin_hi


def _in_proj_kernel(*refs, mix_width, diff):
    if diff:
        (x_ref, g_ref, w_ref, mg_ref, qg_ref, kg_ref, cos_ref, slo_ref, shi_ref,
         q_out, k_out, v_out, m_out) = refs
    else:
        x_ref, g_ref, w_ref, mg_ref, q_out, k_out, v_out, m_out = refs
    h = _rms(x_ref[...], g_ref[...]).astype(BF16)
    ones = _group_ones(LANES, HEAD_DIM)
    chunk = 2 * LANES
    for c in range(mix_width // chunk):
        lo = c * chunk
        q = _dot(h, w_ref[:, lo:lo + chunk])
        k = _dot(h, w_ref[:, mix_width + lo:mix_width + lo + chunk])
        for half in range(2):
            sl = slice(half * LANES, (half + 1) * LANES)
            dst = slice(lo + half * LANES, lo + (half + 1) * LANES)
            qh, kh = q[:, sl], k[:, sl]
            if diff:
                rope = functools.partial(_rope, cos=cos_ref[...], sin_lo=slo_ref[...],
                                         sin_hi=shi_ref[...])
                qh = rope(_head_rms(qh, qg_ref[...], ones))
                kh = rope(_head_rms(kh, kg_ref[...], ones))
            q_out[:, dst] = (qh * (SCALE * LOG2E)).astype(BF16)
            k_out[:, dst] = kh.astype(BF16)
    v_out[...] = _dot(h, w_ref[:, 2 * mix_width:3 * mix_width]).astype(BF16)
    qm = _dot(h, w_ref[:, 3 * mix_width:])
    mem_width = qm.shape[-1]
    for c in range(mem_width // LANES):
        sl = slice(c * LANES, (c + 1) * LANES)
        m_out[:, sl] = (_head_rms(qm[:, sl], mg_ref[...], ones) * SCALE).astype(BF16)


def _in_proj(x2d, g, w, mem_q_gain, diff_args, *, mix_width):
    rows, d_model = x2d.shape
    in_width = w.shape[1]
    mem_width = in_width - 3 * mix_width
    tm = ROW_TILE
    row_spec = lambda n: pl.BlockSpec((tm, n), lambda i: (i, 0))
    const_spec = lambda a: pl.BlockSpec(a.shape, lambda i: (0, 0))
    diff = diff_args is not None
    in_specs = [row_spec(d_model), const_spec(g), const_spec(w), const_spec(mem_q_gain)]
    args = [x2d, g, w, mem_q_gain]
    if diff:
        q_gain, k_gain, cos, sin_lo, sin_hi = diff_args
        in_specs += [const_spec(q_gain), const_spec(k_gain)] + [row_spec(LANES)] * 3
        args += [q_gain, k_gain, cos, sin_lo, sin_hi]
    mix = jax.ShapeDtypeStruct((rows, mix_width), BF16)
    return pl.pallas_call(
        functools.partial(_in_proj_kernel, mix_width=mix_width, diff=diff),
        out_shape=(mix, mix, mix, jax.ShapeDtypeStruct((rows, mem_width), BF16)),
        grid=(rows // tm,),
        in_specs=in_specs,
        out_specs=(row_spec(mix_width), row_spec(mix_width), row_spec(mix_width),
                   row_spec(mem_width)),
        compiler_params=_params("parallel"),
        name="in_proj_diff" if diff else "in_proj_sb",
    )(*args)


def _mem_attn_kernel(q_ref, k_ref, v_ref, o_ref):
    q, k, v = q_ref[0], k_ref[0], v_ref[0]
    lane = lax.broadcasted_iota(jnp.int32, (1, q.shape[-1]), 1) // HEAD_DIM
    out = jnp.zeros(q.shape, F32)
    for hd in range(N_MEM_HEADS):
        sel = lane == hd
        s = _dot_nt(jnp.where(sel, q, 0), k)
        p = jnp.exp(s - jnp.max(s, axis=-1, keepdims=True))
        p = p / jnp.sum(p, axis=-1, keepdims=True)
        out = out + _dot(p.astype(BF16), jnp.where(sel, v, 0))
    o_ref[0] = out.astype(BF16)


def _mem_attn(qm, km, vm):
    b, s, width = qm.shape
    m = km.shape[1]
    tq = MEM_Q_TILE
    return pl.pallas_call(
        _mem_attn_kernel,
        out_shape=jax.ShapeDtypeStruct((b, s, width), BF16),
        grid=(b, s // tq),
        in_specs=[pl.BlockSpec((1, tq, width), lambda bi, i: (bi, i, 0)),
                  pl.BlockSpec((1, m, width), lambda bi, i: (bi, 0, 0)),
                  pl.BlockSpec((1, m, width), lambda bi, i: (bi, 0, 0))],
        out_specs=pl.BlockSpec((1, tq, width), lambda bi, i: (bi, i, 0)),
        compiler_params=_params("parallel", "parallel"),
        name="mem_attn",
    )(qm, km, vm)


def _sb_kernel(q_ref, k_ref, v_ref, o_ref):
    t = q_ref.shape[1]
    i = pl.program_id(2)
    q = q_ref[0]
    lane = lax.broadcasted_iota(jnp.int32, (1, LANES), 1)
    first = lane < HEAD_DIM
    q_heads = (jnp.where(first, q, 0), jnp.where(first, 0, q))
    key = lax.broadcasted_iota(jnp.int32, (t, t), 0)
    qry = lax.broadcasted_iota(jnp.int32, (t, t), 1)
    later = (qry > key).astype(BF16)
    strict = key < qry

    def block(j, carry, mask, live):
        spent_a, spent_b, o = carry
        kb = k_ref[0, pl.ds(j * t, t), :]
        vb = v_ref[0, pl.ds(j * t, t), :]
        v_heads = (jnp.where(first, vb, 0), jnp.where(first, 0, vb))
        spent = [spent_a, spent_b]
        for hd in range(2):
            z = _dot_nt(kb, q_heads[hd])
            sp = jnp.maximum(z, 0.0) + jnp.log2(1.0 + jnp.exp2(-jnp.abs(z)))
            if mask is not None:
                sp = jnp.where(mask, sp, 0.0)
            after = _dot(later, sp.astype(BF16)) + spent[hd]
            w = jnp.exp2(z - sp - after)
            if mask is not None:
                w = jnp.where(mask, w, 0.0)
            if live is not None:
                w = jnp.where(live, w, 0.0)
            o = o + lax.dot_general(v_heads[hd], w.astype(BF16), (((0,), (0,)), ((), ())),
                                    preferred_element_type=F32)
            spent[hd] = spent[hd] + jnp.sum(sp, axis=0, keepdims=True)
        return spent[0], spent[1], o

    def stick_left(spent_a, spent_b):
        return (jnp.min(jnp.minimum(spent_a, spent_b)) < SB_DONE).astype(jnp.int32)

    def earlier(state):
        n, _, carry = state
        carry = block(i - 1 - n, carry, None, None)
        return n + 1, stick_left(carry[0], carry[1]), carry

    zero = jnp.zeros((1, t), F32)
    carry = block(i, (zero, zero, jnp.zeros((LANES, t), F32)), strict, None)
    carry = block(jnp.maximum(i - 1, 0), carry, None, i > 0)
    state = (jnp.int32(1), stick_left(carry[0], carry[1]), carry)
    state = lax.while_loop(lambda st: (st[0] < i) & (st[1] > 0), earlier, state)
    o_ref[0] = state[2][2].T.astype(BF16)


def _token_attn(kernel, name, t, q, k, v, extra=(), scratch=()):
    b, s, width = q.shape
    q_spec = pl.BlockSpec((1, t, LANES), lambda bi, p, i: (bi, i, p))
    kv_spec = pl.BlockSpec((1, s, LANES), lambda bi, p, i: (bi, 0, p))
    extra_specs = [pl.BlockSpec(a.shape, lambda bi, p, i: (0, 0)) for a in extra]
    return pl.pallas_call(
        kernel,
        out_shape=jax.ShapeDtypeStruct((b, s, width), BF16),
        grid=(b, width // LANES, s // t),
        in_specs=[q_spec, kv_spec, kv_spec] + extra_specs,
        out_specs=q_spec,
        scratch_shapes=list(scratch),
        compiler_params=_params("parallel", "parallel", "arbitrary"),
        name=name,
    )(q, k, v, *extra)


def _diff_kernel(q_ref, k_ref, v_ref, lam_ref, g_ref, o_ref, *s_refs, out_scale):
    t = q_ref.shape[1]
    i = pl.program_id(2)
    q = q_ref[0]
    lane = lax.broadcasted_iota(jnp.int32, (1, LANES), 1)
    first = lane < HEAD_DIM
    q_maps = (jnp.where(first, q, 0), jnp.where(first, 0, q))
    key = lax.broadcasted_iota(jnp.int32, (t, t), 0)
    qry = lax.broadcasted_iota(jnp.int32, (t, t), 1)
    causal = key <= qry

    def scores(j, slot, mask):
        kb = k_ref[0, pl.ds(j * t, t), :]
        tops = []
        for c in range(2):
            s = _dot_nt(kb, q_maps[c])
            if mask is not None:
                s = jnp.where(mask, s, NEG_BIG)
            s_refs[slot][c] = s
            tops.append(jnp.max(s, axis=0, keepdims=True))
        return tuple(tops)

    def absorb(j, slot, tops, carry):
        vb = v_ref[0, pl.ds(j * t, t), :]
        new = []
        for c in range(2):
            m, l, acc = carry[c]
            m_new = jnp.maximum(m, tops[c])
            alpha = jnp.exp2(m - m_new)
            p = jnp.exp2(s_refs[slot][c] - m_new)
            l = alpha * l + jnp.sum(p, axis=0, keepdims=True)
            pv = lax.dot_general(vb, p.astype(BF16), (((0,), (0,)), ((), ())),
                                 preferred_element_type=F32)
            new.append((m_new, l, alpha * acc + pv))
        return tuple(new)

    def step(n, slot, state):
        tops, carry = state
        tops_next = scores(n, 1 - slot, None)
        return tops_next, absorb(jnp.where(n == 0, i, n - 1), slot, tops, carry)

    def pair(n2, state):
        return step(2 * n2 + 1, 1, step(2 * n2, 0, state))

    def tail_even(state):
        tops, carry = state
        return absorb(jnp.maximum(i - 1, 0), 0, tops, carry)

    def tail_odd(state):
        tops, carry = step(i - 1, 0, state)
        return absorb(i - 1, 1, tops, carry)

    init = (jnp.full((1, t), NEG_BIG, F32), jnp.zeros((1, t), F32), jnp.zeros((LANES, t), F32))
    state = lax.fori_loop(0, i // 2, pair, (scores(i, 0, causal), (init, init)))
    carry = lax.cond(i % 2 == 1, tail_odd, tail_even, state)
    (_, l1, acc1), (_, l2, acc2) = carry
    o = acc1 / l1 - lam_ref[...] * (acc2 / l2)
    o = o * lax.rsqrt(jnp.mean(o * o, axis=0, keepdims=True) + EPS)
    o_ref[0] = (o.T * (g_ref[...] * out_scale)).astype(BF16)


def _top2_route(logits):
    lane = lax.broadcasted_iota(jnp.int32, logits.shape, 1)
    lg = jnp.where(lane < N_EXPERTS, logits, -jnp.inf)
    m1 = jnp.max(lg, axis=-1, keepdims=True)
    i1 = jnp.min(jnp.where(lg == m1, lane, LANES), axis=-1, keepdims=True)
    lg2 = jnp.where(lane == i1, -jnp.inf, lg)
    m2 = jnp.max(lg2, axis=-1, keepdims=True)
    i2 = jnp.min(jnp.where(lg2 == m2, lane, LANES), axis=-1, keepdims=True)
    e = jnp.exp(m2 - m1)
    w1 = 1.0 / (1.0 + e)
    fields = (i1.astype(F32), i2.astype(F32), w1, e * w1)
    out = jnp.zeros(logits.shape, F32)
    for n, val in enumerate(fields):
        out = jnp.where(lane == n, val, out)
    return out


def _out_proj_kernel(*refs, moe):
    if moe:
        mix_ref, mem_ref, w_ref, x_ref, g_ref, wr_ref, x_out, h_out, c_out = refs
    else:
        mix_ref, mem_ref, w_ref, x_ref, g_ref, x_out, h_out = refs
    mix_width = mix_ref.shape[-1]
    x = x_ref[...] + _dot(mix_ref[...], w_ref[:mix_width, :]) + _dot(mem_ref[...], w_ref[mix_width:, :])
    x_out[...] = x
    h = _rms(x, g_ref[...])
    h_out[...] = h.astype(h_out.dtype)
    if moe:
        h_hi, h_lo = _split_bf16(h)
        w_hi, w_lo = _split_bf16(wr_ref[...])
        c_out[...] = _top2_route(_dot(h_hi, w_hi) + _dot(h_hi, w_lo) + _dot(h_lo, w_hi))


def _out_proj(o_mix, o_mem, w, x2d, g, w_router):
    rows, d_model = x2d.shape
    tm = ROW_TILE
    row_spec = lambda n: pl.BlockSpec((tm, n), lambda i: (i, 0))
    const_spec = lambda a: pl.BlockSpec(a.shape, lambda i: (0, 0))
    moe = w_router is not None
    in_specs = [row_spec(o_mix.shape[1]), row_spec(o_mem.shape[1]), const_spec(w),
                row_spec(d_model), const_spec(g)]
    args = [o_mix, o_mem, w, x2d, g]
    out_shape = [jax.ShapeDtypeStruct((rows, d_model), F32),
                 jax.ShapeDtypeStruct((rows, d_model), F32 if moe else BF16)]
    out_specs = [row_spec(d_model), row_spec(d_model)]
    if moe:
        in_specs.append(const_spec(w_router))
        args.append(w_router)
        out_shape.append(jax.ShapeDtypeStruct((rows, LANES), F32))
        out_specs.append(row_spec(LANES))
    return pl.pallas_call(
        functools.partial(_out_proj_kernel, moe=moe),
        out_shape=tuple(out_shape),
        grid=(rows // tm,),
        in_specs=in_specs,
        out_specs=tuple(out_specs),
        compiler_params=_params("parallel"),
        name="out_proj_moe" if moe else "out_proj",
    )(*args)


def _swiglu_chunk(h, wg, wu, wd):
    g = _dot(h, wg)
    u = _dot(h, wu)
    return _dot((g * jax.nn.sigmoid(g) * u).astype(BF16), wd)


def _dense_ffn_kernel(h_ref, wg_ref, wu_ref, wd_ref, x_ref, o_ref, acc_ref):
    f = pl.program_id(1)

    @pl.when(f == 0)
    def _():
        acc_ref[...] = x_ref[...]

    acc_ref[...] += _swiglu_chunk(h_ref[...], wg_ref[...], wu_ref[...], wd_ref[...])

    @pl.when(f == pl.num_programs(1) - 1)
    def _():
        o_ref[...] = acc_ref[...]


def _dense_ffn(h, w_gate_up, w_down, x2d, *, ff_tile):
    rows, d_model = x2d.shape
    d_ff = w_down.shape[0]
    nf = d_ff // ff_tile
    tm = FFN_ROW_TILE
    return pl.pallas_call(
        _dense_ffn_kernel,
        out_shape=jax.ShapeDtypeStruct((rows, d_model), F32),
        grid=(rows // tm, nf),
        in_specs=[pl.BlockSpec((tm, d_model), lambda i, f: (i, 0)),
                  pl.BlockSpec((d_model, ff_tile), lambda i, f: (0, f)),
                  pl.BlockSpec((d_model, ff_tile), lambda i, f: (0, f + nf)),
                  pl.BlockSpec((ff_tile, d_model), lambda i, f: (f, 0)),
                  pl.BlockSpec((tm, d_model), lambda i, f: (i, 0))],
        out_specs=pl.BlockSpec((tm, d_model), lambda i, f: (i, 0)),
        scratch_shapes=[pltpu.VMEM((tm, d_model), F32)],
        compiler_params=_params("parallel", "arbitrary"),
        name="dense_ffn",
    )(h, w_gate_up, w_gate_up, w_down, x2d)


def _route_plan(route, tm):
    n_tokens = route.shape[0]
    expert = jnp.concatenate([route[:, 0], route[:, 1]]).astype(jnp.int32)
    onehot = (expert[:, None] == jnp.arange(N_EXPERTS, dtype=jnp.int32)[None, :]).astype(jnp.int32)
    csum = jnp.cumsum(onehot, axis=0)
    tiles = (csum[-1] + tm - 1) // tm
    tile_end = jnp.cumsum(tiles)
    start = (tile_end - tiles) * tm
    pos = jnp.sum(onehot * (start[None, :] + csum - 1), axis=1)
    n_tiles = 2 * n_tokens // tm + N_EXPERTS
    tile_expert = jnp.sum(jnp.arange(n_tiles, dtype=jnp.int32)[:, None] >= tile_end[None, :], axis=1)
    n_used = tile_end[-1]
    tile_expert = jnp.minimum(tile_expert, tile_expert[n_used - 1])
    pos = pos.astype(jnp.int32)
    token = jnp.arange(2 * n_tokens, dtype=jnp.int32) % n_tokens
    src = jnp.zeros((n_tiles * tm,), jnp.int32).at[pos].set(token, unique_indices=True)
    return pos, src, jnp.concatenate([tile_expert, n_used[None]]).astype(jnp.int32)


def _for_each_row(n_rows, fn):
    def trip(g, c):
        base = pl.multiple_of(g * DMA_UNROLL, DMA_UNROLL)
        for u in range(DMA_UNROLL):
            fn(base + u)
        return c

    lax.fori_loop(0, n_rows // DMA_UNROLL, trip, 0)


def _expert_ffn_kernel(plan_ref, src_ref, nxt_ref, h_ref, wg_ref, wu_ref, wd_ref, ys_ref,
                       hs_ref, hb_ref, acc_ref, sem, *, nf):
    i, f = pl.program_id(0), pl.program_id(1)
    n_tiles = pl.num_programs(0)
    n_used = plan_ref[n_tiles]
    used = i < n_used
    fetched = (i == 0) | (i - 1 < n_used)
    tm = hs_ref.shape[1]
    share = tm // nf
    slot = i % 2

    def row_copy(idx_ref, r, s):
        return pltpu.make_async_copy(h_ref.at[pl.ds(idx_ref[0, 0, r], 1)],
                                     hs_ref.at[s, pl.ds(r, 1)], sem.at[s])

    @pl.when((i == 0) & (f == 0))
    def _():
        _for_each_row(tm, lambda r: row_copy(src_ref, r, 0).start())

    @pl.when(fetched & (f == 0))
    def _():
        _for_each_row(tm, lambda r: row_copy(src_ref, r, slot).wait())

    @pl.when(used & (f == 0))
    def _():
        hb_ref[...] = hs_ref[slot].astype(BF16)
        acc_ref[...] = jnp.zeros_like(acc_ref)

    @pl.when(used)
    def _():
        for u in range(share):
            row_copy(nxt_ref, f * share + u, 1 - slot).start()
        acc_ref[...] += _swiglu_chunk(hb_ref[...], wg_ref[0], wu_ref[0], wd_ref[0])

    @pl.when(used & (f == nf - 1))
    def _():
        ys_ref[...] = acc_ref[...]

    @pl.when(used & (i == n_tiles - 1) & (f == nf - 1))
    def _():
        _for_each_row(tm, lambda r: row_copy(nxt_ref, r, 1 - slot).wait())

    @pl.when(jnp.logical_not(used) & (f == nf - 1))
    def _():
        ys_ref[...] = jnp.zeros_like(ys_ref)


def _expert_ffn(h, src, plan, w_gate_up, w_down, *, ff_tile):
    d_model = h.shape[1]
    d_ff = w_down.shape[1]
    nf = d_ff // ff_tile
    tm = MOE_ROW_TILE
    n_tiles = plan.shape[0] - 1
    n_rows = n_tiles * tm

    def chunk_of(i, f, plan_ref):
        return jnp.where(i < plan_ref[n_tiles], f, nf - 1)

    grid_spec = pltpu.PrefetchScalarGridSpec(
        num_scalar_prefetch=1,
        grid=(n_tiles, nf),
        in_specs=[
            pl.BlockSpec((1, 1, tm), lambda i, f, p: (i, 0, 0), memory_space=pltpu.SMEM),
            pl.BlockSpec((1, 1, tm), lambda i, f, p: (jnp.minimum(i + 1, n_tiles - 1), 0, 0),
                         memory_space=pltpu.SMEM),
            pl.BlockSpec(memory_space=pl.ANY),
            pl.BlockSpec((1, d_model, ff_tile), lambda i, f, p: (p[i], 0, chunk_of(i, f, p))),
            pl.BlockSpec((1, d_model, ff_tile), lambda i, f, p: (p[i], 0, chunk_of(i, f, p) + nf)),
            pl.BlockSpec((1, ff_tile, d_model), lambda i, f, p: (p[i], chunk_of(i, f, p), 0)),
        ],
        out_specs=pl.BlockSpec((tm, d_model), lambda i, f, p: (i, 0)),
        scratch_shapes=[pltpu.VMEM((2, tm, d_model), F32), pltpu.VMEM((tm, d_model), BF16),
                        pltpu.VMEM((tm, d_model), F32), pltpu.SemaphoreType.DMA((2,))],
    )
    src3 = src.reshape(n_tiles, 1, tm)
    return pl.pallas_call(
        functools.partial(_expert_ffn_kernel, nf=nf),
        out_shape=jax.ShapeDtypeStruct((n_rows, d_model), F32),
        grid_spec=grid_spec,
        compiler_params=_params("arbitrary", "arbitrary"),
        name="moe_experts",
    )(plan, src3, src3, h, w_gate_up, w_gate_up, w_down)


def _combine_kernel(p1_ref, p2_ref, ys_ref, route_ref, x_ref, o_ref, buf_ref, sem):
    chunk = x_ref.shape[0]

    def row_copy(k, pos_ref, r):
        return pltpu.make_async_copy(ys_ref.at[pl.ds(pos_ref[0, 0, r], 1)],
                                     buf_ref.at[k, pl.ds(r, 1)], sem.at[k])

    def start(r):
        row_copy(0, p1_ref, r).start()
        row_copy(1, p2_ref, r).start()

    def wait(r):
        row_copy(0, p1_ref, r).wait()
        row_copy(1, p2_ref, r).wait()

    _for_each_row(chunk, start)
    _for_each_row(chunk, wait)
    route = route_ref[...]
    o_ref[...] = x_ref[...] + route[:, 2:3] * buf_ref[0] + route[:, 3:4] * buf_ref[1]


def _combine(ys, pos, route, x2d):
    rows, d_model = x2d.shape
    chunk = ROUTE_CHUNK
    n_chunks = rows // chunk
    pos3 = pos.reshape(2 * n_chunks, 1, chunk)
    smem_spec = lambda off: pl.BlockSpec((1, 1, chunk), lambda c: (c + off, 0, 0), memory_space=pltpu.SMEM)
    row_spec = lambda n: pl.BlockSpec((chunk, n), lambda c: (c, 0))
    return pl.pallas_call(
        _combine_kernel,
        out_shape=jax.ShapeDtypeStruct((rows, d_model), F32),
        grid=(n_chunks,),
        in_specs=[smem_spec(0), smem_spec(n_chunks), pl.BlockSpec(memory_space=pl.ANY),
                  row_spec(LANES), row_spec(d_model)],
        out_specs=row_spec(d_model),
        scratch_shapes=[pltpu.VMEM((2, chunk, d_model), F32), pltpu.SemaphoreType.DMA((2,))],
        compiler_params=_params("arbitrary"),
        name="moe_combine",
    )(pos3, pos3, ys, route, x2d)


def _moe_ffn(h, route, w_gate_up, w_down, x2d, *, ff_tile):
    pos, src, plan = _route_plan(route, MOE_ROW_TILE)
    ys = _expert_ffn(h, src, plan, w_gate_up, w_down, ff_tile=ff_tile)
    return _combine(ys, pos, route, x2d)


def _rope_tables(positions):
    half = ROPE_DIM // 2
    inv_freq = ROPE_THETA ** (-jnp.arange(0, ROPE_DIM, 2, dtype=F32) / ROPE_DIM)
    ang = positions.astype(F32).reshape(-1, 1) * inv_freq
    cos, sin = jnp.cos(ang), jnp.sin(ang)
    rows = ang.shape[0]
    pad = jnp.zeros((rows, HEAD_DIM - ROPE_DIM), F32)
    zero = jnp.zeros((rows, half), F32)
    cos_h = jnp.concatenate([cos, cos, pad + 1.0], axis=-1)
    lo_h = jnp.concatenate([-sin, zero, pad], axis=-1)
    hi_h = jnp.concatenate([zero, sin, pad], axis=-1)
    tile = lambda a: jnp.concatenate([a] * (LANES // HEAD_DIM), axis=-1)
    return tile(cos_h), tile(lo_h), tile(hi_h)


def _tile_gain(g, width):
    return jnp.tile(g.astype(F32), width // g.shape[-1]).reshape(1, width)


def _ff_tile(d_ff, limit):
    return max(t for t in range(LANES, limit + 1, LANES) if d_ff % t == 0)


def kernel(x, mem, positions, attn_norm, w_in, w_out, mem_norm, w_mem_kv, mem_q_norm, mem_k_norm,
           diff_q_norm, diff_k_norm, diff_lambda, diff_subln, ffn_norm, dense_w_gate_up,
           dense_w_down, w_router, moe_w_gate_up, moe_w_down):
    b, s, d_model = x.shape
    depth = w_in.shape[0]
    mem_len = mem.shape[1]
    mem_width = w_mem_kv.shape[-1] // 2
    mix_width = (w_in.shape[-1] - mem_width) // 3
    rows = b * s
    assert rows % FFN_ROW_TILE == 0 and s % MEM_Q_TILE == 0 and mix_width % (2 * LANES) == 0

    row = lambda a: a.astype(F32).reshape(1, -1)
    k_gain = jnp.stack([_tile_gain(mem_k_norm[i], mem_width) for i in range(depth)])
    km, vm = _mem_kv(mem.reshape(b * mem_len, d_model), row(mem_norm), w_mem_kv.astype(BF16), k_gain)
    km = km.reshape(depth, b, mem_len, mem_width)
    vm = vm.reshape(depth, b, mem_len, mem_width)
    cos, sin_lo, sin_hi = _rope_tables(positions)

    x2d = x.reshape(rows, d_model)
    for i in range(depth):
        j = i // 2
        is_diff = i % 2 == 1
        diff_args = None
        if is_diff:
            diff_args = (_tile_gain(diff_q_norm[j], LANES), _tile_gain(diff_k_norm[j], LANES),
                         cos, sin_lo, sin_hi)
        q, k, v, qm = _in_proj(x2d, row(attn_norm[i]), w_in[i].astype(BF16),
                               _tile_gain(mem_q_norm[i], LANES), diff_args, mix_width=mix_width)
        to3 = lambda a: a.reshape(b, s, a.shape[-1])
        o_mem = _mem_attn(to3(qm), km[i], vm[i])
        if is_diff:
            lam_init = 0.8 - 0.6 * math.exp(-0.3 * i)
            lp = diff_lambda[j].astype(F32)
            lam = jnp.exp(jnp.sum(lp[0] * lp[1])) - jnp.exp(jnp.sum(lp[2] * lp[3])) + lam_init
            o_mix = _token_attn(functools.partial(_diff_kernel, out_scale=1.0 - lam_init), "diff_attn",
                                DIFF_TILE, to3(q), to3(k), to3(v),
                                extra=(lam.reshape(1, 1), row(diff_subln[j])),
                                scratch=(pltpu.VMEM((2, DIFF_TILE, DIFF_TILE), F32),) * 2)
        else:
            o_mix = _token_attn(_sb_kernel, "sb_attn", SB_TILE, to3(q), to3(k), to3(v))
        router = None
        if is_diff:
            router = jnp.pad(w_router[j].astype(F32), ((0, 0), (0, LANES - w_router.shape[-1])))
        outs = _out_proj(o_mix.reshape(rows, mix_width), o_mem.reshape(rows, mem_width),
                         w_out[i].astype(BF16), x2d, row(ffn_norm[i]), router)
        if is_diff:
            x2d, h, route = outs
            x2d = _moe_ffn(h, route, moe_w_gate_up[j].astype(BF16), moe_w_down[j].astype(BF16), x2d,
                           ff_tile=_ff_tile(moe_w_down.shape[-2], 1024))
        else:
            x2d, h = outs
            x2d = _dense_ffn(h, dense_w_gate_up[j].astype(BF16), dense_w_down[j].astype(BF16), x2d,
                             ff_tile=_ff_tile(dense_w_down.shape[-2], 1536))
    return x2d.reshape(b, s, d_model)
```

```python
import functools
import math

import jax
import jax.numpy as jnp
import numpy as np
from jax import lax
from jax.experimental import pallas as pl
from jax.experimental.pallas import tpu as pltpu

F32 = jnp.float32
BF16 = jnp.bfloat16

HEAD_DIM = 64
LANES = 128
MXU_WIDTH = 256
N_MEM_HEADS = 4
ROPE_DIM = HEAD_DIM // 4
ROPE_THETA = 500000.0
N_EXPERTS = 8
EPS = 1e-6
NEG_BIG = -1e30
SCALE = HEAD_DIM ** -0.5
LOG2E = math.log2(math.e)
SB_DONE = 104.0 * LOG2E
VMEM_LIMIT = 48 * 1024 * 1024

ROW_TILE = 512
FFN_ROW_TILE = 1024
SB_TILE = 256
DIFF_TILE = 512
MOE_ROW_TILE = 512
ROUTE_CHUNK = 512
DMA_UNROLL = 8
MEM_Q_TILE = 1024


def _params(*sem):
    return pltpu.CompilerParams(dimension_semantics=sem, vmem_limit_bytes=VMEM_LIMIT)


def _split_bf16(x):
    hi = x.astype(BF16)
    lo = (x - hi.astype(F32)).astype(BF16)
    return hi, lo


def _dot(a, b):
    return jnp.dot(a, b, preferred_element_type=F32)


def _dot_nt(a, b):
    return lax.dot_general(a, b, (((1,), (1,)), ((), ())), preferred_element_type=F32)


def _dot_split(x, m):
    hi, lo = _split_bf16(x)
    return _dot(hi, m) + _dot(lo, m)


def _group_ones(n, group):
    r = lax.broadcasted_iota(jnp.int32, (n, n), 0) // group
    c = lax.broadcasted_iota(jnp.int32, (n, n), 1) // group
    return (r == c).astype(BF16)


def _head_rms(t, gain, ones):
    ss = _dot_split(t * t, ones)
    return t * lax.rsqrt(ss * (1.0 / HEAD_DIM) + EPS) * gain


def _rms(x, g):
    return x * lax.rsqrt(jnp.mean(x * x, axis=-1, keepdims=True) + EPS) * g


def _mem_kv_kernel(mem_ref, g_ref, w_ref, kg_ref, k_out, v_out):
    width = k_out.shape[-1]
    mem_n = _rms(mem_ref[...], g_ref[...]).astype(BF16)
    kv = _dot(mem_n, w_ref[0])
    ones = _group_ones(width, HEAD_DIM)
    k_out[0] = _head_rms(kv[:, :width], kg_ref[0], ones).astype(BF16)
    v_out[0] = kv[:, width:].astype(BF16)


def _mem_kv(mem2d, mem_norm, w_mem_kv, k_gain):
    depth, d_model, two_w = w_mem_kv.shape
    width = two_w // 2
    rows = mem2d.shape[0]
    out = jax.ShapeDtypeStruct((depth, rows, width), BF16)
    return pl.pallas_call(
        _mem_kv_kernel,
        out_shape=(out, out),
        grid=(depth,),
        in_specs=[
            pl.BlockSpec((rows, d_model), lambda i: (0, 0)),
            pl.BlockSpec((1, d_model), lambda i: (0, 0)),
            pl.BlockSpec((1, d_model, two_w), lambda i: (i, 0, 0)),
            pl.BlockSpec((1, 1, width), lambda i: (i, 0, 0)),
        ],
        out_specs=(pl.BlockSpec((1, rows, width), lambda i: (i, 0, 0)),
                   pl.BlockSpec((1, rows, width), lambda i: (i, 0, 0))),
        compiler_params=_params("arbitrary"),
        name="mem_kv",
    )(mem2d, mem_norm, w_mem_kv, k_gain)


def _rope(t, cos, sin_lo, sin_hi):
    n = t.shape[-1]
    half = ROPE_DIM // 2
    return t * cos + pltpu.roll(t, n - half, 1) * sin_lo + pltpu.roll(t, half, 1) * sin_hi


def _in_proj_kernel(*refs, mix_width, diff):
    if diff:
        (x_ref, g_ref, w_ref, mg_ref, qg_ref, kg_ref, cs_ref, spread_ref, offset_ref,
         q_out, k_out, v_out, m_out) = refs
        hi = cs_ref[...].astype(BF16)
        mid, lo = _split_bf16(cs_ref[...] - hi.astype(F32))
        spread = spread_ref[...].astype(BF16)
        tables = _dot(hi, spread) + _dot(mid, spread) + _dot(lo, spread) + offset_ref[...]
        rope = functools.partial(_rope, cos=tables[:, :LANES], sin_lo=tables[:, LANES:2 * LANES],
                                 sin_hi=tables[:, 2 * LANES:])
    else:
        x_ref, g_ref, w_ref, mg_ref, q_out, k_out, v_out, m_out = refs
    h = _rms(x_ref[...], g_ref[...]).astype(BF16)
    ones = _group_ones(LANES, HEAD_DIM)
    chunk = 2 * LANES
    for c in range(mix_width // chunk):
        lo = c * chunk
        q = _dot(h, w_ref[:, lo:lo + chunk])
        k = _dot(h, w_ref[:, mix_width + lo:mix_width + lo + chunk])
        for half in range(2):
            sl = slice(half * LANES, (half + 1) * LANES)
            dst = slice(lo + half * LANES, lo + (half + 1) * LANES)
            qh, kh = q[:, sl], k[:, sl]
            if diff:
                qh = rope(_head_rms(qh, qg_ref[...], ones))
                kh = rope(_head_rms(kh, kg_ref[...], ones))
            q_out[:, dst] = (qh * (SCALE * LOG2E)).astype(BF16)
            k_out[:, dst] = kh.astype(BF16)
    v_out[...] = _dot(h, w_ref[:, 2 * mix_width:3 * mix_width]).astype(BF16)
    qm = _dot(h, w_ref[:, 3 * mix_width:])
    mem_width = qm.shape[-1]
    for c in range(mem_width // LANES):
        sl = slice(c * LANES, (c + 1) * LANES)
        m_out[:, sl] = (_head_rms(qm[:, sl], mg_ref[...], ones) * SCALE).astype(BF16)


def _in_proj(x2d, g, w, mem_q_gain, diff_args, *, mix_width):
    rows, d_model = x2d.shape
    in_width = w.shape[1]
    mem_width = in_width - 3 * mix_width
    tm = ROW_TILE
    row_spec = lambda n: pl.BlockSpec((tm, n), lambda i: (i, 0))
    const_spec = lambda a: pl.BlockSpec(a.shape, lambda i: (0, 0))
    diff = diff_args is not None
    in_specs = [row_spec(d_model), const_spec(g), const_spec(w), const_spec(mem_q_gain)]
    args = [x2d, g, w, mem_q_gain]
    if diff:
        q_gain, k_gain, cos_sin, spread, offset = diff_args
        in_specs += [const_spec(q_gain), const_spec(k_gain), row_spec(cos_sin.shape[1]),
                     const_spec(spread), const_spec(offset)]
        args += [q_gain, k_gain, cos_sin, spread, offset]
    mix = jax.ShapeDtypeStruct((rows, mix_width), BF16)
    return pl.pallas_call(
        functools.partial(_in_proj_kernel, mix_width=mix_width, diff=diff),
        out_shape=(mix, mix, mix, jax.ShapeDtypeStruct((rows, mem_width), BF16)),
        grid=(rows // tm,),
        in_specs=in_specs,
        out_specs=(row_spec(mix_width), row_spec(mix_width), row_spec(mix_width),
                   row_spec(mem_width)),
        compiler_params=_params("parallel"),
        name="in_proj_diff" if diff else "in_proj_sb",
    )(*args)


def _mem_attn_kernel(q_ref, k_ref, v_ref, o_ref):
    q, k, v = q_ref[0], k_ref[0], v_ref[0]
    lane = lax.broadcasted_iota(jnp.int32, (1, q.shape[-1]), 1) // HEAD_DIM
    out = jnp.zeros(q.shape, F32)
    for hd in range(N_MEM_HEADS):
        sel = lane == hd
        s = _dot_nt(jnp.where(sel, q, 0), k)
        p = jnp.exp(s - jnp.max(s, axis=-1, keepdims=True))
        p = p / jnp.sum(p, axis=-1, keepdims=True)
        out = out + _dot(p.astype(BF16), jnp.where(sel, v, 0))
    o_ref[0] = out.astype(BF16)


def _mem_attn(qm, km, vm):
    b, s, width = qm.shape
    m = km.shape[1]
    tq = MEM_Q_TILE
    return pl.pallas_call(
        _mem_attn_kernel,
        out_shape=jax.ShapeDtypeStruct((b, s, width), BF16),
        grid=(b, s // tq),
        in_specs=[pl.BlockSpec((1, tq, width), lambda bi, i: (bi, i, 0)),
                  pl.BlockSpec((1, m, width), lambda bi, i: (bi, 0, 0)),
                  pl.BlockSpec((1, m, width), lambda bi, i: (bi, 0, 0))],
        out_specs=pl.BlockSpec((1, tq, width), lambda bi, i: (bi, i, 0)),
        compiler_params=_params("parallel", "parallel"),
        name="mem_attn",
    )(qm, km, vm)


def _sb_kernel(q_ref, k_ref, v_ref, o_ref):
    t = q_ref.shape[1]
    i = pl.program_id(2)
    q = q_ref[0]
    lane = lax.broadcasted_iota(jnp.int32, (1, LANES), 1)
    first = lane < HEAD_DIM
    q_heads = (jnp.where(first, q, 0), jnp.where(first, 0, q))
    key = lax.broadcasted_iota(jnp.int32, (t, t), 0)
    qry = lax.broadcasted_iota(jnp.int32, (t, t), 1)
    later = (qry > key).astype(BF16)
    strict = key < qry

    def block(j, carry, mask, live):
        spent_a, spent_b, o = carry
        kb = k_ref[0, pl.ds(j * t, t), :]
        vb = v_ref[0, pl.ds(j * t, t), :]
        v_heads = (jnp.where(first, vb, 0), jnp.where(first, 0, vb))
        spent = [spent_a, spent_b]
        for hd in range(2):
            z = _dot_nt(kb, q_heads[hd])
            sp = jnp.maximum(z, 0.0) + jnp.log2(1.0 + jnp.exp2(-jnp.abs(z)))
            if mask is not None:
                sp = jnp.where(mask, sp, 0.0)
            after = _dot(later, sp.astype(BF16)) + spent[hd]
            w = jnp.exp2(z - sp - after)
            if mask is not None:
                w = jnp.where(mask, w, 0.0)
            if live is not None:
                w = jnp.where(live, w, 0.0)
            o = o + lax.dot_general(v_heads[hd], w.astype(BF16), (((0,), (0,)), ((), ())),
                                    preferred_element_type=F32)
            spent[hd] = spent[hd] + jnp.sum(sp, axis=0, keepdims=True)
        return spent[0], spent[1], o

    def stick_left(spent_a, spent_b):
        return (jnp.min(jnp.minimum(spent_a, spent_b)) < SB_DONE).astype(jnp.int32)

    def earlier(state):
        n, _, carry = state
        carry = block(i - 1 - n, carry, None, None)
        return n + 1, stick_left(carry[0], carry[1]), carry

    zero = jnp.zeros((1, t), F32)
    carry = block(i, (zero, zero, jnp.zeros((LANES, t), F32)), strict, None)
    carry = block(jnp.maximum(i - 1, 0), carry, None, i > 0)
    state = (jnp.int32(1), stick_left(carry[0], carry[1]), carry)
    state = lax.while_loop(lambda st: (st[0] < i) & (st[1] > 0), earlier, state)
    o_ref[0] = state[2][2].T.astype(BF16)


def _token_attn(kernel, name, t, q, k, v, extra=(), scratch=()):
    b, s, width = q.shape
    q_spec = pl.BlockSpec((1, t, LANES), lambda bi, p, i: (bi, i, p))
    kv_spec = pl.BlockSpec((1, s, LANES), lambda bi, p, i: (bi, 0, p))
    extra_specs = [pl.BlockSpec(a.shape, lambda bi, p, i: (0, 0)) for a in extra]
    return pl.pallas_call(
        kernel,
        out_shape=jax.ShapeDtypeStruct((b, s, width), BF16),
        grid=(b, width // LANES, s // t),
        in_specs=[q_spec, kv_spec, kv_spec] + extra_specs,
        out_specs=q_spec,
        scratch_shapes=list(scratch),
        compiler_params=_params("parallel", "parallel", "arbitrary"),
        name=name,
    )(q, k, v, *extra)


def _diff_kernel(q_ref, k_ref, v_ref, lam_ref, g_ref, o_ref, *s_refs, out_scale):
    t = q_ref.shape[1]
    i = pl.program_id(2)
    q = q_ref[0]
    lane = lax.broadcasted_iota(jnp.int32, (1, LANES), 1)
    first = lane < HEAD_DIM
    q_maps = (jnp.where(first, q, 0), jnp.where(first, 0, q))
    key = lax.broadcasted_iota(jnp.int32, (t, t), 0)
    qry = lax.broadcasted_iota(jnp.int32, (t, t), 1)
    causal = key <= qry

    def scores(j, slot, mask):
        kb = k_ref[0, pl.ds(j * t, t), :]
        tops = []
        for c in range(2):
            s = _dot_nt(kb, q_maps[c])
            if mask is not None:
                s = jnp.where(mask, s, NEG_BIG)
            s_refs[slot][c] = s
            tops.append(jnp.max(s, axis=0, keepdims=True))
        return tuple(tops)

    def absorb(j, slot, tops, carry):
        vb = v_ref[0, pl.ds(j * t, t), :]
        new = []
        for c in range(2):
            m, l, acc = carry[c]
            m_new = jnp.maximum(m, tops[c])
            alpha = jnp.exp2(m - m_new)
            p = jnp.exp2(s_refs[slot][c] - m_new)
            l = alpha * l + jnp.sum(p, axis=0, keepdims=True)
            pv = lax.dot_general(vb, p.astype(BF16), (((0,), (0,)), ((), ())),
                                 preferred_element_type=F32)
            new.append((m_new, l, alpha * acc + pv))
        return tuple(new)

    def step(n, slot, state):
        tops, carry = state
        tops_next = scores(n, 1 - slot, None)
        return tops_next, absorb(jnp.where(n == 0, i, n - 1), slot, tops, carry)

    def pair(n2, state):
        return step(2 * n2 + 1, 1, step(2 * n2, 0, state))

    def tail_even(state):
        tops, carry = state
        return absorb(jnp.maximum(i - 1, 0), 0, tops, carry)

    def tail_odd(state):
        tops, carry = step(i - 1, 0, state)
        return absorb(i - 1, 1, tops, carry)

    init = (jnp.full((1, t), NEG_BIG, F32), jnp.zeros((1, t), F32), jnp.zeros((LANES, t), F32))
    state = lax.fori_loop(0, i // 2, pair, (scores(i, 0, causal), (init, init)))
    carry = lax.cond(i % 2 == 1, tail_odd, tail_even, state)
    (_, l1, acc1), (_, l2, acc2) = carry
    o = acc1 / l1 - lam_ref[...] * (acc2 / l2)
    o = o * lax.rsqrt(jnp.mean(o * o, axis=0, keepdims=True) + EPS)
    o_ref[0] = (o.T * (g_ref[...] * out_scale)).astype(BF16)


def _top2_route(logits):
    lane = lax.broadcasted_iota(jnp.int32, logits.shape, 1)
    lg = jnp.where(lane < N_EXPERTS, logits, -jnp.inf)
    m1 = jnp.max(lg, axis=-1, keepdims=True)
    i1 = jnp.min(jnp.where(lg == m1, lane, LANES), axis=-1, keepdims=True)
    lg2 = jnp.where(lane == i1, -jnp.inf, lg)
    m2 = jnp.max(lg2, axis=-1, keepdims=True)
    i2 = jnp.min(jnp.where(lg2 == m2, lane, LANES), axis=-1, keepdims=True)
    e = jnp.exp(m2 - m1)
    w1 = 1.0 / (1.0 + e)
    fields = (i1.astype(F32), i2.astype(F32), w1, e * w1)
    out = jnp.zeros(logits.shape, F32)
    for n, val in enumerate(fields):
        out = jnp.where(lane == n, val, out)
    return out


def _out_proj_kernel(*refs, moe):
    if moe:
        mix_ref, mem_ref, w_ref, x_ref, g_ref, wr_ref, x_out, h_out, c_out = refs
    else:
        mix_ref, mem_ref, w_ref, x_ref, g_ref, x_out, h_out = refs
    mix_width = mix_ref.shape[-1]
    x = x_ref[...] + _dot(mix_ref[...], w_ref[:mix_width, :]) + _dot(mem_ref[...], w_ref[mix_width:, :])
    x_out[...] = x
    h = _rms(x, g_ref[...])
    h_out[...] = h.astype(h_out.dtype)
    if moe:
        h_hi, h_lo = _split_bf16(h)
        w_hi, w_lo = _split_bf16(wr_ref[...])
        c_out[...] = _top2_route(_dot(h_hi, w_hi) + _dot(h_hi, w_lo) + _dot(h_lo, w_hi))


def _out_proj(o_mix, o_mem, w, x2d, g, w_router):
    rows, d_model = x2d.shape
    tm = ROW_TILE
    row_spec = lambda n: pl.BlockSpec((tm, n), lambda i: (i, 0))
    const_spec = lambda a: pl.BlockSpec(a.shape, lambda i: (0, 0))
    moe = w_router is not None
    in_specs = [row_spec(o_mix.shape[1]), row_spec(o_mem.shape[1]), const_spec(w),
                row_spec(d_model), const_spec(g)]
    args = [o_mix, o_mem, w, x2d, g]
    out_shape = [jax.ShapeDtypeStruct((rows, d_model), F32),
                 jax.ShapeDtypeStruct((rows, d_model), F32 if moe else BF16)]
    out_specs = [row_spec(d_model), row_spec(d_model)]
    if moe:
        in_specs.append(const_spec(w_router))
        args.append(w_router)
        out_shape.append(jax.ShapeDtypeStruct((rows, LANES), F32))
        out_specs.append(row_spec(LANES))
    return pl.pallas_call(
        functools.partial(_out_proj_kernel, moe=moe),
        out_shape=tuple(out_shape),
        grid=(rows // tm,),
        in_specs=in_specs,
        out_specs=tuple(out_specs),
        compiler_params=_params("parallel"),
        name="out_proj_moe" if moe else "out_proj",
    )(*args)


def _swiglu_chunk(h, wg, wu, wd):
    g = _dot(h, wg)
    u = _dot(h, wu)
    return _dot((g * jax.nn.sigmoid(g) * u).astype(BF16), wd)


def _dense_ffn_kernel(h_ref, wg_ref, wu_ref, wd_ref, x_ref, o_ref, acc_ref):
    f = pl.program_id(1)

    @pl.when(f == 0)
    def _():
        acc_ref[...] = x_ref[...]

    acc_ref[...] += _swiglu_chunk(h_ref[...], wg_ref[...], wu_ref[...], wd_ref[...])

    @pl.when(f == pl.num_programs(1) - 1)
    def _():
        o_ref[...] = acc_ref[...]


def _dense_ffn(h, w_gate_up, w_down, x2d, *, ff_tile):
    rows, d_model = x2d.shape
    d_ff = w_down.shape[0]
    nf = d_ff // ff_tile
    tm = FFN_ROW_TILE
    return pl.pallas_call(
        _dense_ffn_kernel,
        out_shape=jax.ShapeDtypeStruct((rows, d_model), F32),
        grid=(rows // tm, nf),
        in_specs=[pl.BlockSpec((tm, d_model), lambda i, f: (i, 0)),
                  pl.BlockSpec((d_model, ff_tile), lambda i, f: (0, f)),
                  pl.BlockSpec((d_model, ff_tile), lambda i, f: (0, f + nf)),
                  pl.BlockSpec((ff_tile, d_model), lambda i, f: (f, 0)),
                  pl.BlockSpec((tm, d_model), lambda i, f: (i, 0))],
        out_specs=pl.BlockSpec((tm, d_model), lambda i, f: (i, 0)),
        scratch_shapes=[pltpu.VMEM((tm, d_model), F32)],
        compiler_params=_params("parallel", "arbitrary"),
        name="dense_ffn",
    )(h, w_gate_up, w_gate_up, w_down, x2d)


def _route_plan(route, tm):
    n_tokens = route.shape[0]
    expert = jnp.concatenate([route[:, 0], route[:, 1]]).astype(jnp.int32)
    onehot = (expert[:, None] == jnp.arange(N_EXPERTS, dtype=jnp.int32)[None, :]).astype(jnp.int32)
    csum = jnp.cumsum(onehot, axis=0)
    tiles = (csum[-1] + tm - 1) // tm
    tile_end = jnp.cumsum(tiles)
    start = (tile_end - tiles) * tm
    pos = jnp.sum(onehot * (start[None, :] + csum - 1), axis=1)
    n_tiles = 2 * n_tokens // tm + N_EXPERTS
    tile_expert = jnp.sum(jnp.arange(n_tiles, dtype=jnp.int32)[:, None] >= tile_end[None, :], axis=1)
    n_used = tile_end[-1]
    tile_expert = jnp.minimum(tile_expert, tile_expert[n_used - 1])
    pos = pos.astype(jnp.int32)
    token = jnp.arange(2 * n_tokens, dtype=jnp.int32) % n_tokens
    src = jnp.zeros((n_tiles * tm,), jnp.int32).at[pos].set(token, unique_indices=True)
    return pos, src, jnp.concatenate([tile_expert, n_used[None]]).astype(jnp.int32)


def _for_each_row(n_rows, fn):
    def trip(g, c):
        base = pl.multiple_of(g * DMA_UNROLL, DMA_UNROLL)
        for u in range(DMA_UNROLL):
            fn(base + u)
        return c

    lax.fori_loop(0, n_rows // DMA_UNROLL, trip, 0)


def _expert_ffn_kernel(plan_ref, src_ref, nxt_ref, h_ref, wg_ref, wu_ref, wd_ref, ys_ref,
                       hs_ref, hb_ref, acc_ref, sem, *, nf):
    i, f = pl.program_id(0), pl.program_id(1)
    n_tiles = pl.num_programs(0)
    n_used = plan_ref[n_tiles]
    used = i < n_used
    fetched = (i == 0) | (i - 1 < n_used)
    tm = hs_ref.shape[1]
    share = tm // nf
    slot = i % 2

    def row_copy(idx_ref, r, s):
        return pltpu.make_async_copy(h_ref.at[pl.ds(idx_ref[0, 0, r], 1)],
                                     hs_ref.at[s, pl.ds(r, 1)], sem.at[s])

    @pl.when((i == 0) & (f == 0))
    def _():
        _for_each_row(tm, lambda r: row_copy(src_ref, r, 0).start())

    @pl.when(fetched & (f == 0))
    def _():
        _for_each_row(tm, lambda r: row_copy(src_ref, r, slot).wait())

    @pl.when(used & (f == 0))
    def _():
        hb_ref[...] = hs_ref[slot].astype(BF16)
        acc_ref[...] = jnp.zeros_like(acc_ref)

    @pl.when(used)
    def _():
        for u in range(share):
            row_copy(nxt_ref, f * share + u, 1 - slot).start()
        acc_ref[...] += _swiglu_chunk(hb_ref[...], wg_ref[0], wu_ref[0], wd_ref[0])

    @pl.when(used & (f == nf - 1))
    def _():
        ys_ref[...] = acc_ref[...]

    @pl.when(used & (i == n_tiles - 1) & (f == nf - 1))
    def _():
        _for_each_row(tm, lambda r: row_copy(nxt_ref, r, 1 - slot).wait())

    @pl.when(jnp.logical_not(used) & (f == nf - 1))
    def _():
        ys_ref[...] = jnp.zeros_like(ys_ref)


def _expert_ffn(h, src, plan, w_gate_up, w_down, *, ff_tile):
    d_model = h.shape[1]
    d_ff = w_down.shape[1]
    nf = d_ff // ff_tile
    tm = MOE_ROW_TILE
    n_tiles = plan.shape[0] - 1
    n_rows = n_tiles * tm

    def chunk_of(i, f, plan_ref):
        return jnp.where(i < plan_ref[n_tiles], f, nf - 1)

    grid_spec = pltpu.PrefetchScalarGridSpec(
        num_scalar_prefetch=1,
        grid=(n_tiles, nf),
        in_specs=[
            pl.BlockSpec((1, 1, tm), lambda i, f, p: (i, 0, 0), memory_space=pltpu.SMEM),
            pl.BlockSpec((1, 1, tm), lambda i, f, p: (jnp.minimum(i + 1, n_tiles - 1), 0, 0),
                         memory_space=pltpu.SMEM),
            pl.BlockSpec(memory_space=pl.ANY),
            pl.BlockSpec((1, d_model, ff_tile), lambda i, f, p: (p[i], 0, chunk_of(i, f, p))),
            pl.BlockSpec((1, d_model, ff_tile), lambda i, f, p: (p[i], 0, chunk_of(i, f, p) + nf)),
            pl.BlockSpec((1, ff_tile, d_model), lambda i, f, p: (p[i], chunk_of(i, f, p), 0)),
        ],
        out_specs=pl.BlockSpec((tm, d_model), lambda i, f, p: (i, 0)),
        scratch_shapes=[pltpu.VMEM((2, tm, d_model), F32), pltpu.VMEM((tm, d_model), BF16),
                        pltpu.VMEM((tm, d_model), F32), pltpu.SemaphoreType.DMA((2,))],
    )
    src3 = src.reshape(n_tiles, 1, tm)
    return pl.pallas_call(
        functools.partial(_expert_ffn_kernel, nf=nf),
        out_shape=jax.ShapeDtypeStruct((n_rows, d_model), F32),
        grid_spec=grid_spec,
        compiler_params=_params("arbitrary", "arbitrary"),
        name="moe_experts",
    )(plan, src3, src3, h, w_gate_up, w_gate_up, w_down)


def _combine_kernel(p1_ref, p2_ref, ys_ref, route_ref, x_ref, o_ref, buf_ref, sem):
    chunk = x_ref.shape[0]

    def row_copy(k, pos_ref, r):
        return pltpu.make_async_copy(ys_ref.at[pl.ds(pos_ref[0, 0, r], 1)],
                                     buf_ref.at[k, pl.ds(r, 1)], sem.at[k])

    def start(r):
        row_copy(0, p1_ref, r).start()
        row_copy(1, p2_ref, r).start()

    def wait(r):
        row_copy(0, p1_ref, r).wait()
        row_copy(1, p2_ref, r).wait()

    _for_each_row(chunk, start)
    _for_each_row(chunk, wait)
    route = route_ref[...]
    o_ref[...] = x_ref[...] + route[:, 2:3] * buf_ref[0] + route[:, 3:4] * buf_ref[1]


def _combine(ys, pos, route, x2d):
    rows, d_model = x2d.shape
    chunk = ROUTE_CHUNK
    n_chunks = rows // chunk
    pos3 = pos.reshape(2 * n_chunks, 1, chunk)
    smem_spec = lambda off: pl.BlockSpec((1, 1, chunk), lambda c: (c + off, 0, 0), memory_space=pltpu.SMEM)
    row_spec = lambda n: pl.BlockSpec((chunk, n), lambda c: (c, 0))
    return pl.pallas_call(
        _combine_kernel,
        out_shape=jax.ShapeDtypeStruct((rows, d_model), F32),
        grid=(n_chunks,),
        in_specs=[smem_spec(0), smem_spec(n_chunks), pl.BlockSpec(memory_space=pl.ANY),
                  row_spec(LANES), row_spec(d_model)],
        out_specs=row_spec(d_model),
        scratch_shapes=[pltpu.VMEM((2, chunk, d_model), F32), pltpu.SemaphoreType.DMA((2,))],
        compiler_params=_params("arbitrary"),
        name="moe_combine",
    )(pos3, pos3, ys, route, x2d)


def _moe_ffn(h, route, w_gate_up, w_down, x2d, *, ff_tile):
    pos, src, plan = _route_plan(route, MOE_ROW_TILE)
    ys = _expert_ffn(h, src, plan, w_gate_up, w_down, ff_tile=ff_tile)
    return _combine(ys, pos, route, x2d)


def _rope_tables(positions):
    half = ROPE_DIM // 2
    inv_freq = ROPE_THETA ** (-jnp.arange(0, ROPE_DIM, 2, dtype=F32) / ROPE_DIM)
    ang = positions.astype(F32).reshape(-1, 1) * inv_freq
    cos_sin = jnp.concatenate([jnp.cos(ang), jnp.sin(ang)], axis=-1)
    spread = np.zeros((ROPE_DIM, 3 * LANES), np.float32)
    offset = np.zeros((1, 3 * LANES), np.float32)
    for lane in range(LANES):
        m = lane % HEAD_DIM
        if m < ROPE_DIM:
            spread[m % half, lane] = 1.0
        else:
            offset[0, lane] = 1.0
        if m < half:
            spread[half + m, LANES + lane] = -1.0
        elif m < ROPE_DIM:
            spread[half + m - half, 2 * LANES + lane] = 1.0
    return cos_sin, jnp.asarray(spread), jnp.asarray(offset)


def _tile_gain(g, width):
    return jnp.tile(g.astype(F32), width // g.shape[-1]).reshape(1, width)


def _ff_tile(d_ff, limit):
    for step in (MXU_WIDTH, LANES):
        fits = [t for t in range(step, limit + 1, step) if d_ff % t == 0 and t > step]
        if fits:
            return max(fits)
    return LANES


def kernel(x, mem, positions, attn_norm, w_in, w_out, mem_norm, w_mem_kv, mem_q_norm, mem_k_norm,
           diff_q_norm, diff_k_norm, diff_lambda, diff_subln, ffn_norm, dense_w_gate_up,
           dense_w_down, w_router, moe_w_gate_up, moe_w_down):
    b, s, d_model = x.shape
    depth = w_in.shape[0]
    mem_len = mem.shape[1]
    mem_width = w_mem_kv.shape[-1] // 2
    mix_width = (w_in.shape[-1] - mem_width) // 3
    rows = b * s
    assert rows % FFN_ROW_TILE == 0 and s % MEM_Q_TILE == 0 and mix_width % (2 * LANES) == 0

    row = lambda a: a.astype(F32).reshape(1, -1)
    k_gain = jnp.stack([_tile_gain(mem_k_norm[i], mem_width) for i in range(depth)])
    km, vm = _mem_kv(mem.reshape(b * mem_len, d_model), row(mem_norm), w_mem_kv.astype(BF16), k_gain)
    km = km.reshape(depth, b, mem_len, mem_width)
    vm = vm.reshape(depth, b, mem_len, mem_width)
    rope_tables = _rope_tables(positions)

    x2d = x.reshape(rows, d_model)
    for i in range(depth):
        j = i // 2
        is_diff = i % 2 == 1
        diff_args = None
        if is_diff:
            diff_args = (_tile_gain(diff_q_norm[j], LANES), _tile_gain(diff_k_norm[j], LANES),
                         *rope_tables)
        q, k, v, qm = _in_proj(x2d, row(attn_norm[i]), w_in[i].astype(BF16),
                               _tile_gain(mem_q_norm[i], LANES), diff_args, mix_width=mix_width)
        to3 = lambda a: a.reshape(b, s, a.shape[-1])
        o_mem = _mem_attn(to3(qm), km[i], vm[i])
        if is_diff:
            lam_init = 0.8 - 0.6 * math.exp(-0.3 * i)
            lp = diff_lambda[j].astype(F32)
            lam = jnp.exp(jnp.sum(lp[0] * lp[1])) - jnp.exp(jnp.sum(lp[2] * lp[3])) + lam_init
            o_mix = _token_attn(functools.partial(_diff_kernel, out_scale=1.0 - lam_init), "diff_attn",
                                DIFF_TILE, to3(q), to3(k), to3(v),
                                extra=(lam.reshape(1, 1), row(diff_subln[j])),
                                scratch=(pltpu.VMEM((2, DIFF_TILE, DIFF_TILE), F32),) * 2)
        else:
            o_mix = _token_attn(_sb_kernel, "sb_attn", SB_TILE, to3(q), to3(k), to3(v))
        router = None
        if is_diff:
            router = jnp.pad(w_router[j].astype(F32), ((0, 0), (0, LANES - w_router.shape[-1])))
        outs = _out_proj(o_mix.reshape(rows, mix_width), o_mem.reshape(rows, mem_width),
                         w_out[i].astype(BF16), x2d, row(ffn_norm[i]), router)
        if is_diff:
            x2d, h, route = outs
            x2d = _moe_ffn(h, route, moe_w_gate_up[j].astype(BF16), moe_w_down[j].astype(BF16), x2d,
                           ff_tile=_ff_tile(moe_w_down.shape[-2], 2048))
        else:
            x2d, h = outs
            x2d = _dense_ffn(h, dense_w_gate_up[j].astype(BF16), dense_w_down[j].astype(BF16), x2d,
                             ff_tile=_ff_tile(dense_w_down.shape[-2], 1536))
    return x2d.reshape(b, s, d_model)
```

```python
import functools
import math

import jax
import jax.numpy as jnp
import numpy as np
from jax import lax
from jax.experimental import pallas as pl
from jax.experimental.pallas import tpu as pltpu

F32 = jnp.float32
BF16 = jnp.bfloat16

HEAD_DIM = 64
LANES = 128
MXU_WIDTH = 256
N_MEM_HEADS = 4
ROPE_DIM = HEAD_DIM // 4
ROPE_THETA = 500000.0
N_EXPERTS = 8
EPS = 1e-6
NEG_BIG = -1e30
SCALE = HEAD_DIM ** -0.5
LOG2E = math.log2(math.e)
SB_DONE = 104.0 * LOG2E
VMEM_LIMIT = 48 * 1024 * 1024

ROW_TILE = 512
FFN_ROW_TILE = 512
SB_TILE = 256
DIFF_TILE = 512
MOE_ROW_TILE = 512
ROUTE_CHUNK = 512
DMA_UNROLL = 8


def _params(*sem):
    return pltpu.CompilerParams(dimension_semantics=sem, vmem_limit_bytes=VMEM_LIMIT)


def _layer_spec(stacked, layer, **kwargs):
    return pl.BlockSpec((None,) + stacked.shape[1:], lambda i: (layer, 0, 0), **kwargs)


def _split_bf16(x):
    hi = x.astype(BF16)
    lo = (x - hi.astype(F32)).astype(BF16)
    return hi, lo


def _dot(a, b):
    return jnp.dot(a, b, preferred_element_type=F32)


def _dot_nt(a, b):
    return lax.dot_general(a, b, (((1,), (1,)), ((), ())), preferred_element_type=F32)


def _dot_split(x, m):
    hi, lo = _split_bf16(x)
    return _dot(hi, m) + _dot(lo, m)


def _group_ones(n, group):
    r = lax.broadcasted_iota(jnp.int32, (n, n), 0) // group
    c = lax.broadcasted_iota(jnp.int32, (n, n), 1) // group
    return (r == c).astype(BF16)


def _head_rms(t, gain, ones):
    ss = _dot_split(t * t, ones)
    return t * lax.rsqrt(ss * (1.0 / HEAD_DIM) + EPS) * gain


def _rms(x, g):
    return x * lax.rsqrt(jnp.mean(x * x, axis=-1, keepdims=True) + EPS) * g


def _mem_kv_kernel(mem_ref, g_ref, w_ref, kg_ref, k_out, v_out):
    width = k_out.shape[-1]
    mem_n = _rms(mem_ref[...], g_ref[...]).astype(BF16)
    kv = _dot(mem_n, w_ref[0])
    ones = _group_ones(width, HEAD_DIM)
    k_out[0] = _head_rms(kv[:, :width], kg_ref[0], ones).astype(BF16)
    v_out[0] = kv[:, width:].astype(BF16)


def _mem_kv(mem2d, mem_norm, w_mem_kv, k_gain):
    depth, d_model, two_w = w_mem_kv.shape
    width = two_w // 2
    rows = mem2d.shape[0]
    out = jax.ShapeDtypeStruct((depth, rows, width), BF16)
    return pl.pallas_call(
        _mem_kv_kernel,
        out_shape=(out, out),
        grid=(depth,),
        in_specs=[
            pl.BlockSpec((rows, d_model), lambda i: (0, 0)),
            pl.BlockSpec((1, d_model), lambda i: (0, 0)),
            pl.BlockSpec((1, d_model, two_w), lambda i: (i, 0, 0)),
            pl.BlockSpec((1, 1, width), lambda i: (i, 0, 0)),
        ],
        out_specs=(pl.BlockSpec((1, rows, width), lambda i: (i, 0, 0)),
                   pl.BlockSpec((1, rows, width), lambda i: (i, 0, 0))),
        compiler_params=_params("arbitrary"),
        name="mem_kv",
    )(mem2d, mem_norm, w_mem_kv, k_gain)


def _rope(t, cos, sin_lo, sin_hi):
    n = t.shape[-1]
    half = ROPE_DIM // 2
    return t * cos + pltpu.roll(t, n - half, 1) * sin_lo + pltpu.roll(t, half, 1) * sin_hi


def _in_proj_kernel(*refs, mix_width, diff):
    if diff:
        (x_ref, g_ref, w_ref, mg_ref, qg_ref, kg_ref, cs_ref, spread_ref, offset_ref,
         q_out, k_out, v_out, m_out) = refs
        hi = cs_ref[...].astype(BF16)
        mid, lo = _split_bf16(cs_ref[...] - hi.astype(F32))
        spread = spread_ref[...].astype(BF16)
        tables = _dot(hi, spread) + _dot(mid, spread) + _dot(lo, spread) + offset_ref[...]
        rope = functools.partial(_rope, cos=tables[:, :LANES], sin_lo=tables[:, LANES:2 * LANES],
                                 sin_hi=tables[:, 2 * LANES:])
    else:
        x_ref, g_ref, w_ref, mg_ref, q_out, k_out, v_out, m_out = refs
    h = _rms(x_ref[...], g_ref[...]).astype(BF16)
    ones = _group_ones(LANES, HEAD_DIM)
    chunk = 2 * LANES
    for c in range(mix_width // chunk):
        lo = c * chunk
        q = _dot(h, w_ref[:, lo:lo + chunk])
        k = _dot(h, w_ref[:, mix_width + lo:mix_width + lo + chunk])
        for half in range(2):
            sl = slice(half * LANES, (half + 1) * LANES)
            dst = slice(lo + half * LANES, lo + (half + 1) * LANES)
            qh, kh = q[:, sl], k[:, sl]
            if diff:
                qh = rope(_head_rms(qh, qg_ref[...], ones))
                kh = rope(_head_rms(kh, kg_ref[...], ones))
            q_out[:, dst] = (qh * (SCALE * LOG2E)).astype(BF16)
            k_out[:, dst] = kh.astype(BF16)
    v_out[...] = _dot(h, w_ref[:, 2 * mix_width:3 * mix_width]).astype(BF16)
    qm = _dot(h, w_ref[:, 3 * mix_width:])
    mem_width = qm.shape[-1]
    for c in range(mem_width // LANES):
        sl = slice(c * LANES, (c + 1) * LANES)
        m_out[:, sl] = (_head_rms(qm[:, sl], mg_ref[...], ones) * SCALE).astype(BF16)


def _in_proj(x2d, g, w, layer, mem_q_gain, diff_args, *, mix_width):
    rows, d_model = x2d.shape
    in_width = w.shape[2]
    mem_width = in_width - 3 * mix_width
    tm = ROW_TILE
    row_spec = lambda n: pl.BlockSpec((tm, n), lambda i: (i, 0))
    const_spec = lambda a: pl.BlockSpec(a.shape, lambda i: (0, 0))
    diff = diff_args is not None
    in_specs = [row_spec(d_model), const_spec(g), _layer_spec(w, layer), const_spec(mem_q_gain)]
    args = [x2d, g, w, mem_q_gain]
    if diff:
        q_gain, k_gain, cos_sin, spread, offset = diff_args
        in_specs += [const_spec(q_gain), const_spec(k_gain), row_spec(cos_sin.shape[1]),
                     const_spec(spread), const_spec(offset)]
        args += [q_gain, k_gain, cos_sin, spread, offset]
    mix = jax.ShapeDtypeStruct((rows, mix_width), BF16)
    return pl.pallas_call(
        functools.partial(_in_proj_kernel, mix_width=mix_width, diff=diff),
        out_shape=(mix, mix, mix, jax.ShapeDtypeStruct((rows, mem_width), BF16)),
        grid=(rows // tm,),
        in_specs=in_specs,
        out_specs=(row_spec(mix_width), row_spec(mix_width), row_spec(mix_width),
                   row_spec(mem_width)),
        compiler_params=_params("parallel"),
        name="in_proj_diff" if diff else "in_proj_sb",
    )(*args)


def _mem_attn(q, k, v):
    lane = lax.broadcasted_iota(jnp.int32, (1, q.shape[-1]), 1) // HEAD_DIM
    out = jnp.zeros(q.shape, F32)
    for hd in range(N_MEM_HEADS):
        sel = lane == hd
        s = _dot_nt(jnp.where(sel, q, 0), k)
        p = jnp.exp(s - jnp.max(s, axis=-1, keepdims=True))
        p = p / jnp.sum(p, axis=-1, keepdims=True)
        out = out + _dot(p.astype(BF16), jnp.where(sel, v, 0))
    return out.astype(BF16)


def _sb_kernel(q_ref, k_ref, v_ref, o_ref):
    t = q_ref.shape[1]
    i = pl.program_id(2)
    q = q_ref[0]
    lane = lax.broadcasted_iota(jnp.int32, (1, LANES), 1)
    first = lane < HEAD_DIM
    q_heads = (jnp.where(first, q, 0), jnp.where(first, 0, q))
    key = lax.broadcasted_iota(jnp.int32, (t, t), 0)
    qry = lax.broadcasted_iota(jnp.int32, (t, t), 1)
    later = (qry > key).astype(BF16)
    strict = key < qry

    def block(j, carry, mask, live):
        spent_a, spent_b, o = carry
        kb = k_ref[0, pl.ds(j * t, t), :]
        vb = v_ref[0, pl.ds(j * t, t), :]
        v_heads = (jnp.where(first, vb, 0), jnp.where(first, 0, vb))
        spent = [spent_a, spent_b]
        for hd in range(2):
            z = _dot_nt(kb, q_heads[hd])
            sp = jnp.maximum(z, 0.0) + jnp.log2(1.0 + jnp.exp2(-jnp.abs(z)))
            if mask is not None:
                sp = jnp.where(mask, sp, 0.0)
            after = _dot(later, sp.astype(BF16)) + spent[hd]
            w = jnp.exp2(z - sp - after)
            if mask is not None:
                w = jnp.where(mask, w, 0.0)
            if live is not None:
                w = jnp.where(live, w, 0.0)
            o = o + lax.dot_general(v_heads[hd], w.astype(BF16), (((0,), (0,)), ((), ())),
                                    preferred_element_type=F32)
            spent[hd] = spent[hd] + jnp.sum(sp, axis=0, keepdims=True)
        return spent[0], spent[1], o

    def stick_left(spent_a, spent_b):
        return (jnp.min(jnp.minimum(spent_a, spent_b)) < SB_DONE).astype(jnp.int32)

    def earlier(state):
        n, _, carry = state
        carry = block(i - 1 - n, carry, None, None)
        return n + 1, stick_left(carry[0], carry[1]), carry

    zero = jnp.zeros((1, t), F32)
    carry = block(i, (zero, zero, jnp.zeros((LANES, t), F32)), strict, None)
    carry = block(jnp.maximum(i - 1, 0), carry, None, i > 0)
    state = (jnp.int32(1), stick_left(carry[0], carry[1]), carry)
    state = lax.while_loop(lambda st: (st[0] < i) & (st[1] > 0), earlier, state)
    o_ref[0] = state[2][2].T.astype(BF16)


def _token_attn(kernel, name, t, q, k, v, extra=(), scratch=()):
    b, s, width = q.shape
    q_spec = pl.BlockSpec((1, t, LANES), lambda bi, p, i: (bi, i, p))
    kv_spec = pl.BlockSpec((1, s, LANES), lambda bi, p, i: (bi, 0, p))
    extra_specs = [pl.BlockSpec(a.shape, lambda bi, p, i: (0, 0)) for a in extra]
    return pl.pallas_call(
        kernel,
        out_shape=jax.ShapeDtypeStruct((b, s, width), BF16),
        grid=(b, width // LANES, s // t),
        in_specs=[q_spec, kv_spec, kv_spec] + extra_specs,
        out_specs=q_spec,
        scratch_shapes=list(scratch),
        compiler_params=_params("parallel", "parallel", "arbitrary"),
        name=name,
    )(q, k, v, *extra)


def _diff_kernel(q_ref, k_ref, v_ref, lam_ref, g_ref, o_ref, *s_refs, out_scale):
    t = q_ref.shape[1]
    i = pl.program_id(2)
    q = q_ref[0]
    lane = lax.broadcasted_iota(jnp.int32, (1, LANES), 1)
    first = lane < HEAD_DIM
    q_maps = (jnp.where(first, q, 0), jnp.where(first, 0, q))
    key = lax.broadcasted_iota(jnp.int32, (t, t), 0)
    qry = lax.broadcasted_iota(jnp.int32, (t, t), 1)
    causal = key <= qry

    def scores(j, slot, mask):
        kb = k_ref[0, pl.ds(j * t, t), :]
        tops = []
        for c in range(2):
            s = _dot_nt(kb, q_maps[c])
            if mask is not None:
                s = jnp.where(mask, s, NEG_BIG)
            s_refs[slot][c] = s
            tops.append(jnp.max(s, axis=0, keepdims=True))
        return tuple(tops)

    def absorb(j, slot, tops, carry):
        vb = v_ref[0, pl.ds(j * t, t), :]
        new = []
        for c in range(2):
            m, l, acc = carry[c]
            m_new = jnp.maximum(m, tops[c])
            alpha = jnp.exp2(m - m_new)
            p = jnp.exp2(s_refs[slot][c] - m_new)
            l = alpha * l + jnp.sum(p, axis=0, keepdims=True)
            pv = lax.dot_general(vb, p.astype(BF16), (((0,), (0,)), ((), ())),
                                 preferred_element_type=F32)
            new.append((m_new, l, alpha * acc + pv))
        return tuple(new)

    def step(n, slot, state):
        tops, carry = state
        tops_next = scores(n, 1 - slot, None)
        return tops_next, absorb(jnp.where(n == 0, i, n - 1), slot, tops, carry)

    def pair(n2, state):
        return step(2 * n2 + 1, 1, step(2 * n2, 0, state))

    def tail_even(state):
        tops, carry = state
        return absorb(jnp.maximum(i - 1, 0), 0, tops, carry)

    def tail_odd(state):
        tops, carry = step(i - 1, 0, state)
        return absorb(i - 1, 1, tops, carry)

    init = (jnp.full((1, t), NEG_BIG, F32), jnp.zeros((1, t), F32), jnp.zeros((LANES, t), F32))
    state = lax.fori_loop(0, i // 2, pair, (scores(i, 0, causal), (init, init)))
    carry = lax.cond(i % 2 == 1, tail_odd, tail_even, state)
    (_, l1, acc1), (_, l2, acc2) = carry
    o = acc1 / l1 - lam_ref[...] * (acc2 / l2)
    o = o * lax.rsqrt(jnp.mean(o * o, axis=0, keepdims=True) + EPS)
    o_ref[0] = (o.T * (g_ref[...] * out_scale)).astype(BF16)


def _top2_route(logits):
    lane = lax.broadcasted_iota(jnp.int32, logits.shape, 1)
    lg = jnp.where(lane < N_EXPERTS, logits, -jnp.inf)
    m1 = jnp.max(lg, axis=-1, keepdims=True)
    i1 = jnp.min(jnp.where(lg == m1, lane, LANES), axis=-1, keepdims=True)
    lg2 = jnp.where(lane == i1, -jnp.inf, lg)
    m2 = jnp.max(lg2, axis=-1, keepdims=True)
    i2 = jnp.min(jnp.where(lg2 == m2, lane, LANES), axis=-1, keepdims=True)
    e = jnp.exp(m2 - m1)
    w1 = 1.0 / (1.0 + e)
    fields = (i1.astype(F32), i2.astype(F32), w1, e * w1)
    out = jnp.zeros(logits.shape, F32)
    for n, val in enumerate(fields):
        out = jnp.where(lane == n, val, out)
    return out


def _out_proj_kernel(*refs, moe):
    if moe:
        mix_ref, qm_ref, km_ref, vm_ref, w_ref, x_ref, g_ref, wr_ref, x_out, h_out, c_out = refs
    else:
        mix_ref, qm_ref, km_ref, vm_ref, w_ref, x_ref, g_ref, x_out, h_out = refs
    mix_width = mix_ref.shape[-1]
    o_mem = _mem_attn(qm_ref[...], km_ref[...], vm_ref[...])
    x = x_ref[...] + _dot(mix_ref[...], w_ref[:mix_width, :]) + _dot(o_mem, w_ref[mix_width:, :])
    x_out[...] = x
    h = _rms(x, g_ref[...])
    h_out[...] = h.astype(h_out.dtype)
    if moe:
        h_hi, h_lo = _split_bf16(h)
        w_hi, w_lo = _split_bf16(wr_ref[...])
        c_out[...] = _top2_route(_dot(h_hi, w_hi) + _dot(h_hi, w_lo) + _dot(h_lo, w_hi))


def _out_proj(o_mix, qm, km, vm, w, layer, x2d, g, w_router):
    rows, d_model = x2d.shape
    tm = ROW_TILE
    seq = rows // km.shape[1]
    row_spec = lambda n: pl.BlockSpec((tm, n), lambda i: (i, 0))
    const_spec = lambda a: pl.BlockSpec(a.shape, lambda i: (0, 0))
    mem_spec = pl.BlockSpec((None, None) + km.shape[2:], lambda i: (layer, i * tm // seq, 0, 0))
    moe = w_router is not None
    in_specs = [row_spec(o_mix.shape[1]), row_spec(qm.shape[1]), mem_spec, mem_spec,
                _layer_spec(w, layer), row_spec(d_model), const_spec(g)]
    args = [o_mix, qm, km, vm, w, x2d, g]
    out_shape = [jax.ShapeDtypeStruct((rows, d_model), F32),
                 jax.ShapeDtypeStruct((rows, d_model), F32 if moe else BF16)]
    out_specs = [row_spec(d_model), row_spec(d_model)]
    if moe:
        in_specs.append(const_spec(w_router))
        args.append(w_router)
        out_shape.append(jax.ShapeDtypeStruct((rows, LANES), F32))
        out_specs.append(row_spec(LANES))
    return pl.pallas_call(
        functools.partial(_out_proj_kernel, moe=moe),
        out_shape=tuple(out_shape),
        grid=(rows // tm,),
        in_specs=in_specs,
        out_specs=tuple(out_specs),
        compiler_params=_params("parallel"),
        name="out_proj_moe" if moe else "out_proj",
    )(*args)


def _swiglu_chunk(h, wg, wu, wd):
    g = _dot(h, wg)
    u = _dot(h, wu)
    return _dot((g * jax.nn.sigmoid(g) * u).astype(BF16), wd)


def _dense_ffn_kernel(h_ref, wgu_ref, wd_ref, x_ref, o_ref):
    d_ff = wd_ref.shape[0]
    o_ref[...] = x_ref[...] + _swiglu_chunk(h_ref[...], wgu_ref[:, :d_ff], wgu_ref[:, d_ff:],
                                            wd_ref[...])


def _dense_ffn(h, w_gate_up, w_down, layer, x2d):
    rows, d_model = x2d.shape
    tm = FFN_ROW_TILE
    resident = lambda a: _layer_spec(a, layer, pipeline_mode=pl.Buffered(1))
    row_spec = pl.BlockSpec((tm, d_model), lambda i: (i, 0))
    return pl.pallas_call(
        _dense_ffn_kernel,
        out_shape=jax.ShapeDtypeStruct((rows, d_model), F32),
        grid=(rows // tm,),
        in_specs=[row_spec, resident(w_gate_up), resident(w_down), row_spec],
        out_specs=row_spec,
        compiler_params=_params("parallel"),
        name="dense_ffn",
    )(h, w_gate_up, w_down, x2d)


def _route_plan(route, tm):
    n_tokens = route.shape[0]
    expert = jnp.concatenate([route[:, 0], route[:, 1]]).astype(jnp.int32)
    onehot = (expert[:, None] == jnp.arange(N_EXPERTS, dtype=jnp.int32)[None, :]).astype(jnp.int32)
    csum = jnp.cumsum(onehot, axis=0)
    tiles = (csum[-1] + tm - 1) // tm
    tile_end = jnp.cumsum(tiles)
    start = (tile_end - tiles) * tm
    pos = jnp.sum(onehot * (start[None, :] + csum - 1), axis=1)
    n_tiles = 2 * n_tokens // tm + N_EXPERTS
    tile_expert = jnp.sum(jnp.arange(n_tiles, dtype=jnp.int32)[:, None] >= tile_end[None, :], axis=1)
    n_used = tile_end[-1]
    tile_expert = jnp.minimum(tile_expert, tile_expert[n_used - 1])
    pos = pos.astype(jnp.int32)
    token = jnp.arange(2 * n_tokens, dtype=jnp.int32) % n_tokens
    src = jnp.zeros((n_tiles * tm,), jnp.int32).at[pos].set(token, unique_indices=True)
    return pos, src, jnp.concatenate([tile_expert, n_used[None]]).astype(jnp.int32)


def _for_each_row(n_rows, fn):
    def trip(g, c):
        base = pl.multiple_of(g * DMA_UNROLL, DMA_UNROLL)
        for u in range(DMA_UNROLL):
            fn(base + u)
        return c

    lax.fori_loop(0, n_rows // DMA_UNROLL, trip, 0)


def _expert_ffn_kernel(plan_ref, src_ref, nxt_ref, h_ref, wg_ref, wu_ref, wd_ref, ys_ref,
                       hs_ref, hb_ref, acc_ref, sem, *, nf):
    i, f = pl.program_id(0), pl.program_id(1)
    n_tiles = pl.num_programs(0)
    n_used = plan_ref[n_tiles]
    used = i < n_used
    fetched = (i == 0) | (i - 1 < n_used)
    tm = hs_ref.shape[1]
    share = tm // nf
    slot = i % 2

    def row_copy(idx_ref, r, s):
        return pltpu.make_async_copy(h_ref.at[pl.ds(idx_ref[0, 0, r], 1)],
                                     hs_ref.at[s, pl.ds(r, 1)], sem.at[s])

    @pl.when((i == 0) & (f == 0))
    def _():
        _for_each_row(tm, lambda r: row_copy(src_ref, r, 0).start())

    @pl.when(fetched & (f == 0))
    def _():
        _for_each_row(tm, lambda r: row_copy(src_ref, r, slot).wait())

    @pl.when(used & (f == 0))
    def _():
        hb_ref[...] = hs_ref[slot].astype(BF16)
        acc_ref[...] = jnp.zeros_like(acc_ref)

    @pl.when(used)
    def _():
        for u in range(share):
            row_copy(nxt_ref, f * share + u, 1 - slot).start()
        acc_ref[...] += _swiglu_chunk(hb_ref[...], wg_ref[0], wu_ref[0], wd_ref[0])

    @pl.when(used & (f == nf - 1))
    def _():
        ys_ref[...] = acc_ref[...]

    @pl.when(used & (i == n_tiles - 1) & (f == nf - 1))
    def _():
        _for_each_row(tm, lambda r: row_copy(nxt_ref, r, 1 - slot).wait())

    @pl.when(jnp.logical_not(used) & (f == nf - 1))
    def _():
        ys_ref[...] = jnp.zeros_like(ys_ref)


def _expert_ffn(h, src, plan, w_gate_up, w_down, layer, *, ff_tile):
    d_model = h.shape[1]
    d_ff = w_down.shape[2]
    nf = d_ff // ff_tile
    tm = MOE_ROW_TILE
    n_tiles = plan.shape[0] - 1
    n_rows = n_tiles * tm

    def chunk_of(i, f, plan_ref):
        return jnp.where(i < plan_ref[n_tiles], f, nf - 1)

    grid_spec = pltpu.PrefetchScalarGridSpec(
        num_scalar_prefetch=1,
        grid=(n_tiles, nf),
        in_specs=[
            pl.BlockSpec((1, 1, tm), lambda i, f, p: (i, 0, 0), memory_space=pltpu.SMEM),
            pl.BlockSpec((1, 1, tm), lambda i, f, p: (jnp.minimum(i + 1, n_tiles - 1), 0, 0),
                         memory_space=pltpu.SMEM),
            pl.BlockSpec(memory_space=pl.ANY),
            pl.BlockSpec((None, 1, d_model, ff_tile),
                         lambda i, f, p: (layer, p[i], 0, chunk_of(i, f, p))),
            pl.BlockSpec((None, 1, d_model, ff_tile),
                         lambda i, f, p: (layer, p[i], 0, chunk_of(i, f, p) + nf)),
            pl.BlockSpec((None, 1, ff_tile, d_model),
                         lambda i, f, p: (layer, p[i], chunk_of(i, f, p), 0)),
        ],
        out_specs=pl.BlockSpec((tm, d_model), lambda i, f, p: (i, 0)),
        scratch_shapes=[pltpu.VMEM((2, tm, d_model), F32), pltpu.VMEM((tm, d_model), BF16),
                        pltpu.VMEM((tm, d_model), F32), pltpu.SemaphoreType.DMA((2,))],
    )
    src3 = src.reshape(n_tiles, 1, tm)
    return pl.pallas_call(
        functools.partial(_expert_ffn_kernel, nf=nf),
        out_shape=jax.ShapeDtypeStruct((n_rows, d_model), F32),
        grid_spec=grid_spec,
        compiler_params=_params("arbitrary", "arbitrary"),
        name="moe_experts",
    )(plan, src3, src3, h, w_gate_up, w_gate_up, w_down)


def _combine_kernel(p1_ref, p2_ref, ys_ref, route_ref, x_ref, o_ref, buf_ref, sem):
    chunk = x_ref.shape[0]

    def row_copy(k, pos_ref, r):
        return pltpu.make_async_copy(ys_ref.at[pl.ds(pos_ref[0, 0, r], 1)],
                                     buf_ref.at[k, pl.ds(r, 1)], sem.at[k])

    def start(r):
        row_copy(0, p1_ref, r).start()
        row_copy(1, p2_ref, r).start()

    def wait(r):
        row_copy(0, p1_ref, r).wait()
        row_copy(1, p2_ref, r).wait()

    _for_each_row(chunk, start)
    _for_each_row(chunk, wait)
    route = route_ref[...]
    o_ref[...] = x_ref[...] + route[:, 2:3] * buf_ref[0] + route[:, 3:4] * buf_ref[1]


def _combine(ys, pos, route, x2d):
    rows, d_model = x2d.shape
    chunk = ROUTE_CHUNK
    n_chunks = rows // chunk
    pos3 = pos.reshape(2 * n_chunks, 1, chunk)
    smem_spec = lambda off: pl.BlockSpec((1, 1, chunk), lambda c: (c + off, 0, 0), memory_space=pltpu.SMEM)
    row_spec = lambda n: pl.BlockSpec((chunk, n), lambda c: (c, 0))
    return pl.pallas_call(
        _combine_kernel,
        out_shape=jax.ShapeDtypeStruct((rows, d_model), F32),
        grid=(n_chunks,),
        in_specs=[smem_spec(0), smem_spec(n_chunks), pl.BlockSpec(memory_space=pl.ANY),
                  row_spec(LANES), row_spec(d_model)],
        out_specs=row_spec(d_model),
        scratch_shapes=[pltpu.VMEM((2, chunk, d_model), F32), pltpu.SemaphoreType.DMA((2,))],
        compiler_params=_params("arbitrary"),
        name="moe_combine",
    )(pos3, pos3, ys, route, x2d)


def _moe_ffn(h, route, w_gate_up, w_down, layer, x2d, *, ff_tile):
    pos, src, plan = _route_plan(route, MOE_ROW_TILE)
    ys = _expert_ffn(h, src, plan, w_gate_up, w_down, layer, ff_tile=ff_tile)
    return _combine(ys, pos, route, x2d)


def _rope_tables(positions):
    half = ROPE_DIM // 2
    inv_freq = ROPE_THETA ** (-jnp.arange(0, ROPE_DIM, 2, dtype=F32) / ROPE_DIM)
    ang = positions.astype(F32).reshape(-1, 1) * inv_freq
    cos_sin = jnp.concatenate([jnp.cos(ang), jnp.sin(ang)], axis=-1)
    spread = np.zeros((ROPE_DIM, 3 * LANES), np.float32)
    offset = np.zeros((1, 3 * LANES), np.float32)
    for lane in range(LANES):
        m = lane % HEAD_DIM
        if m < ROPE_DIM:
            spread[m % half, lane] = 1.0
        else:
            offset[0, lane] = 1.0
        if m < half:
            spread[half + m, LANES + lane] = -1.0
        elif m < ROPE_DIM:
            spread[half + m - half, 2 * LANES + lane] = 1.0
    return cos_sin, jnp.asarray(spread), jnp.asarray(offset)


def _tile_gain(g, width):
    return jnp.tile(g.astype(F32), width // g.shape[-1]).reshape(1, width)


def _ff_tile(d_ff, limit):
    for step in (MXU_WIDTH, LANES):
        fits = [t for t in range(step, limit + 1, step) if d_ff % t == 0 and t > step]
        if fits:
            return max(fits)
    return LANES


def kernel(x, mem, positions, attn_norm, w_in, w_out, mem_norm, w_mem_kv, mem_q_norm, mem_k_norm,
           diff_q_norm, diff_k_norm, diff_lambda, diff_subln, ffn_norm, dense_w_gate_up,
           dense_w_down, w_router, moe_w_gate_up, moe_w_down):
    b, s, d_model = x.shape
    depth = w_in.shape[0]
    mem_len = mem.shape[1]
    mem_width = w_mem_kv.shape[-1] // 2
    mix_width = (w_in.shape[-1] - mem_width) // 3
    rows = b * s
    assert s % max(ROW_TILE, FFN_ROW_TILE, SB_TILE, DIFF_TILE) == 0 and mix_width % (2 * LANES) == 0

    row = lambda a: a.astype(F32).reshape(1, -1)
    k_gain = jnp.stack([_tile_gain(mem_k_norm[i], mem_width) for i in range(depth)])
    km, vm = _mem_kv(mem.reshape(b * mem_len, d_model), row(mem_norm), w_mem_kv.astype(BF16), k_gain)
    km = km.reshape(depth, b, mem_len, mem_width)
    vm = vm.reshape(depth, b, mem_len, mem_width)
    rope_tables = _rope_tables(positions)
    w_in, w_out, dense_w_gate_up, dense_w_down, moe_w_gate_up, moe_w_down = (
        w.astype(BF16) for w in (w_in, w_out, dense_w_gate_up, dense_w_down, moe_w_gate_up, moe_w_down))

    x2d = x.reshape(rows, d_model)
    for i in range(depth):
        j = i // 2
        is_diff = i % 2 == 1
        diff_args = None
        if is_diff:
            diff_args = (_tile_gain(diff_q_norm[j], LANES), _tile_gain(diff_k_norm[j], LANES),
                         *rope_tables)
        q, k, v, qm = _in_proj(x2d, row(attn_norm[i]), w_in, i,
                               _tile_gain(mem_q_norm[i], LANES), diff_args, mix_width=mix_width)
        to3 = lambda a: a.reshape(b, s, a.shape[-1])
        if is_diff:
            lam_init = 0.8 - 0.6 * math.exp(-0.3 * i)
            lp = diff_lambda[j].astype(F32)
            lam = jnp.exp(jnp.sum(lp[0] * lp[1])) - jnp.exp(jnp.sum(lp[2] * lp[3])) + lam_init
            o_mix = _token_attn(functools.partial(_diff_kernel, out_scale=1.0 - lam_init), "diff_attn",
                                DIFF_TILE, to3(q), to3(k), to3(v),
                                extra=(lam.reshape(1, 1), row(diff_subln[j])),
                                scratch=(pltpu.VMEM((2, DIFF_TILE, DIFF_TILE), F32),) * 2)
        else:
            o_mix = _token_attn(_sb_kernel, "sb_attn", SB_TILE, to3(q), to3(k), to3(v))
        router = None
        if is_diff:
            router = jnp.pad(w_router[j].astype(F32), ((0, 0), (0, LANES - w_router.shape[-1])))
        outs = _out_proj(o_mix.reshape(rows, mix_width), qm, km, vm, w_out, i, x2d,
                         row(ffn_norm[i]), router)
        if is_diff:
            x2d, h, route = outs
            x2d = _moe_ffn(h, route, moe_w_gate_up, moe_w_down, j, x2d,
                           ff_tile=_ff_tile(moe_w_down.shape[-2], 2048))
        else:
            x2d, h = outs
            x2d = _dense_ffn(h, dense_w_gate_up, dense_w_down, j, x2d)
    return x2d.reshape(b, s, d_model)
```

```python
import functools
import math

import jax
import jax.numpy as jnp
import numpy as np
from jax import lax
from jax.experimental import pallas as pl
from jax.experimental.pallas import tpu as pltpu

F32 = jnp.float32
BF16 = jnp.bfloat16

HEAD_DIM = 64
LANES = 128
MXU_WIDTH = 256
N_MEM_HEADS = 4
ROPE_DIM = HEAD_DIM // 4
ROPE_THETA = 500000.0
N_EXPERTS = 8
EPS = 1e-6
NEG_BIG = -1e30
SCALE = HEAD_DIM ** -0.5
LOG2E = math.log2(math.e)
SB_DONE = 104.0 * LOG2E
VMEM_LIMIT = 48 * 1024 * 1024

ROW_TILE = 512
FFN_ROW_TILE = 512
SB_TILE = 256
DIFF_TILE = 1024
MOE_ROW_TILE = 512
ROUTE_CHUNK = 512
DMA_UNROLL = 8


def _params(*sem):
    return pltpu.CompilerParams(dimension_semantics=sem, vmem_limit_bytes=VMEM_LIMIT)


def _layer_spec(stacked, layer, **kwargs):
    return pl.BlockSpec((None,) + stacked.shape[1:], lambda i: (layer, 0, 0), **kwargs)


def _split_bf16(x):
    hi = x.astype(BF16)
    lo = (x - hi.astype(F32)).astype(BF16)
    return hi, lo


def _dot(a, b):
    return jnp.dot(a, b, preferred_element_type=F32)


def _dot_nt(a, b):
    return lax.dot_general(a, b, (((1,), (1,)), ((), ())), preferred_element_type=F32)


def _dot_split(x, m):
    hi, lo = _split_bf16(x)
    return _dot(hi, m) + _dot(lo, m)


def _group_ones(n, group):
    r = lax.broadcasted_iota(jnp.int32, (n, n), 0) // group
    c = lax.broadcasted_iota(jnp.int32, (n, n), 1) // group
    return (r == c).astype(BF16)


def _head_rms(t, gain, ones):
    ss = _dot_split(t * t, ones)
    return t * lax.rsqrt(ss * (1.0 / HEAD_DIM) + EPS) * gain


def _rms(x, g):
    return x * lax.rsqrt(jnp.mean(x * x, axis=-1, keepdims=True) + EPS) * g


def _mem_kv_kernel(mem_ref, g_ref, w_ref, kg_ref, k_out, v_out):
    width = k_out.shape[-1]
    mem_n = _rms(mem_ref[...], g_ref[...]).astype(BF16)
    kv = _dot(mem_n, w_ref[0])
    ones = _group_ones(width, HEAD_DIM)
    k_out[0] = _head_rms(kv[:, :width], kg_ref[0], ones).astype(BF16)
    v_out[0] = kv[:, width:].astype(BF16)


def _mem_kv(mem2d, mem_norm, w_mem_kv, k_gain):
    depth, d_model, two_w = w_mem_kv.shape
    width = two_w // 2
    rows = mem2d.shape[0]
    out = jax.ShapeDtypeStruct((depth, rows, width), BF16)
    return pl.pallas_call(
        _mem_kv_kernel,
        out_shape=(out, out),
        grid=(depth,),
        in_specs=[
            pl.BlockSpec((rows, d_model), lambda i: (0, 0)),
            pl.BlockSpec((1, d_model), lambda i: (0, 0)),
            pl.BlockSpec((1, d_model, two_w), lambda i: (i, 0, 0)),
            pl.BlockSpec((1, 1, width), lambda i: (i, 0, 0)),
        ],
        out_specs=(pl.BlockSpec((1, rows, width), lambda i: (i, 0, 0)),
                   pl.BlockSpec((1, rows, width), lambda i: (i, 0, 0))),
        compiler_params=_params("arbitrary"),
        name="mem_kv",
    )(mem2d, mem_norm, w_mem_kv, k_gain)


def _rope(t, cos, sin_lo, sin_hi):
    n = t.shape[-1]
    half = ROPE_DIM // 2
    return t * cos + pltpu.roll(t, n - half, 1) * sin_lo + pltpu.roll(t, half, 1) * sin_hi


def _in_proj_kernel(*refs, mix_width, diff):
    if diff:
        (x_ref, g_ref, w_ref, mg_ref, qg_ref, kg_ref, cs_ref, spread_ref, offset_ref,
         q_out, k_out, v_out, m_out) = refs
        hi = cs_ref[...].astype(BF16)
        mid, lo = _split_bf16(cs_ref[...] - hi.astype(F32))
        spread = spread_ref[...].astype(BF16)
        tables = _dot(hi, spread) + _dot(mid, spread) + _dot(lo, spread) + offset_ref[...]
        rope = functools.partial(_rope, cos=tables[:, :LANES], sin_lo=tables[:, LANES:2 * LANES],
                                 sin_hi=tables[:, 2 * LANES:])
    else:
        x_ref, g_ref, w_ref, mg_ref, q_out, k_out, v_out, m_out = refs
    h = _rms(x_ref[...], g_ref[...]).astype(BF16)
    ones = _group_ones(LANES, HEAD_DIM)
    chunk = 2 * LANES
    for c in range(mix_width // chunk):
        lo = c * chunk
        q = _dot(h, w_ref[:, lo:lo + chunk])
        k = _dot(h, w_ref[:, mix_width + lo:mix_width + lo + chunk])
        for half in range(2):
            sl = slice(half * LANES, (half + 1) * LANES)
            dst = slice(lo + half * LANES, lo + (half + 1) * LANES)
            qh, kh = q[:, sl], k[:, sl]
            if diff:
                qh = rope(_head_rms(qh, qg_ref[...], ones))
                kh = rope(_head_rms(kh, kg_ref[...], ones))
            q_out[:, dst] = (qh * (SCALE * LOG2E)).astype(BF16)
            k_out[:, dst] = kh.astype(BF16)
    v_out[...] = _dot(h, w_ref[:, 2 * mix_width:3 * mix_width]).astype(BF16)
    qm = _dot(h, w_ref[:, 3 * mix_width:])
    mem_width = qm.shape[-1]
    for c in range(mem_width // LANES):
        sl = slice(c * LANES, (c + 1) * LANES)
        m_out[:, sl] = (_head_rms(qm[:, sl], mg_ref[...], ones) * SCALE).astype(BF16)


def _in_proj(x2d, g, w, layer, mem_q_gain, diff_args, *, mix_width):
    rows, d_model = x2d.shape
    in_width = w.shape[2]
    mem_width = in_width - 3 * mix_width
    tm = ROW_TILE
    row_spec = lambda n: pl.BlockSpec((tm, n), lambda i: (i, 0))
    const_spec = lambda a: pl.BlockSpec(a.shape, lambda i: (0, 0))
    diff = diff_args is not None
    in_specs = [row_spec(d_model), const_spec(g), _layer_spec(w, layer), const_spec(mem_q_gain)]
    args = [x2d, g, w, mem_q_gain]
    if diff:
        q_gain, k_gain, cos_sin, spread, offset = diff_args
        in_specs += [const_spec(q_gain), const_spec(k_gain), row_spec(cos_sin.shape[1]),
                     const_spec(spread), const_spec(offset)]
        args += [q_gain, k_gain, cos_sin, spread, offset]
    mix = jax.ShapeDtypeStruct((rows, mix_width), BF16)
    return pl.pallas_call(
        functools.partial(_in_proj_kernel, mix_width=mix_width, diff=diff),
        out_shape=(mix, mix, mix, jax.ShapeDtypeStruct((rows, mem_width), BF16)),
        grid=(rows // tm,),
        in_specs=in_specs,
        out_specs=(row_spec(mix_width), row_spec(mix_width), row_spec(mix_width),
                   row_spec(mem_width)),
        compiler_params=_params("parallel"),
        name="in_proj_diff" if diff else "in_proj_sb",
    )(*args)


def _mem_attn(q, k, v):
    lane = lax.broadcasted_iota(jnp.int32, (1, q.shape[-1]), 1) // HEAD_DIM
    out = jnp.zeros(q.shape, F32)
    for hd in range(N_MEM_HEADS):
        sel = lane == hd
        s = _dot_nt(jnp.where(sel, q, 0), k)
        p = jnp.exp(s - jnp.max(s, axis=-1, keepdims=True))
        p = p / jnp.sum(p, axis=-1, keepdims=True)
        out = out + _dot(p.astype(BF16), jnp.where(sel, v, 0))
    return out.astype(BF16)


def _sb_kernel(q_ref, k_ref, v_ref, o_ref):
    t = q_ref.shape[1]
    i = pl.program_id(2)
    q = q_ref[0]
    lane = lax.broadcasted_iota(jnp.int32, (1, LANES), 1)
    first = lane < HEAD_DIM
    q_heads = (jnp.where(first, q, 0), jnp.where(first, 0, q))
    key = lax.broadcasted_iota(jnp.int32, (t, t), 0)
    qry = lax.broadcasted_iota(jnp.int32, (t, t), 1)
    later = (qry > key).astype(BF16)
    strict = key < qry

    def block(j, carry, mask, live):
        spent_a, spent_b, o = carry
        kb = k_ref[0, pl.ds(j * t, t), :]
        vb = v_ref[0, pl.ds(j * t, t), :]
        v_heads = (jnp.where(first, vb, 0), jnp.where(first, 0, vb))
        spent = [spent_a, spent_b]
        for hd in range(2):
            z = _dot_nt(kb, q_heads[hd])
            sp = jnp.maximum(z, 0.0) + jnp.log2(1.0 + jnp.exp2(-jnp.abs(z)))
            if mask is not None:
                sp = jnp.where(mask, sp, 0.0)
            after = _dot(later, sp.astype(BF16)) + spent[hd]
            w = jnp.exp2(z - sp - after)
            if mask is not None:
                w = jnp.where(mask, w, 0.0)
            if live is not None:
                w = jnp.where(live, w, 0.0)
            o = o + lax.dot_general(v_heads[hd], w.astype(BF16), (((0,), (0,)), ((), ())),
                                    preferred_element_type=F32)
            spent[hd] = spent[hd] + jnp.sum(sp, axis=0, keepdims=True)
        return spent[0], spent[1], o

    def stick_left(spent_a, spent_b):
        return (jnp.min(jnp.minimum(spent_a, spent_b)) < SB_DONE).astype(jnp.int32)

    def earlier(state):
        n, _, carry = state
        carry = block(i - 1 - n, carry, None, None)
        return n + 1, stick_left(carry[0], carry[1]), carry

    zero = jnp.zeros((1, t), F32)
    carry = block(i, (zero, zero, jnp.zeros((LANES, t), F32)), strict, None)
    carry = block(jnp.maximum(i - 1, 0), carry, None, i > 0)
    state = (jnp.int32(1), stick_left(carry[0], carry[1]), carry)
    state = lax.while_loop(lambda st: (st[0] < i) & (st[1] > 0), earlier, state)
    o_ref[0] = state[2][2].T.astype(BF16)


def _token_attn(kernel, name, t, q, k, v, extra=(), scratch=()):
    b, s, width = q.shape
    q_spec = pl.BlockSpec((1, t, LANES), lambda bi, p, i: (bi, i, p))
    kv_spec = pl.BlockSpec((1, s, LANES), lambda bi, p, i: (bi, 0, p))
    extra_specs = [pl.BlockSpec(a.shape, lambda bi, p, i: (0, 0)) for a in extra]
    return pl.pallas_call(
        kernel,
        out_shape=jax.ShapeDtypeStruct((b, s, width), BF16),
        grid=(b, width // LANES, s // t),
        in_specs=[q_spec, kv_spec, kv_spec] + extra_specs,
        out_specs=q_spec,
        scratch_shapes=list(scratch),
        compiler_params=_params("parallel", "parallel", "arbitrary"),
        name=name,
    )(q, k, v, *extra)


def _diff_kernel(q_ref, k_ref, v_ref, lam_ref, g_ref, o_ref, *s_refs, out_scale):
    t = q_ref.shape[1]
    i = pl.program_id(2)
    q = q_ref[0]
    lane = lax.broadcasted_iota(jnp.int32, (1, LANES), 1)
    first = lane < HEAD_DIM
    q_maps = (jnp.where(first, q, 0), jnp.where(first, 0, q))
    key = lax.broadcasted_iota(jnp.int32, (t, t), 0)
    qry = lax.broadcasted_iota(jnp.int32, (t, t), 1)
    causal = key <= qry

    def scores(j, slot, mask):
        kb = k_ref[0, pl.ds(j * t, t), :]
        tops = []
        for c in range(2):
            s = _dot_nt(kb, q_maps[c])
            if mask is not None:
                s = jnp.where(mask, s, NEG_BIG)
            s_refs[slot][c] = s
            tops.append(jnp.max(s, axis=0, keepdims=True))
        return tuple(tops)

    def absorb(j, slot, tops, carry):
        vb = v_ref[0, pl.ds(j * t, t), :]
        new = []
        for c in range(2):
            m, l, acc = carry[c]
            m_new = jnp.maximum(m, tops[c])
            alpha = jnp.exp2(m - m_new)
            p = jnp.exp2(s_refs[slot][c] - m_new)
            l = alpha * l + jnp.sum(p, axis=0, keepdims=True)
            pv = lax.dot_general(vb, p.astype(BF16), (((0,), (0,)), ((), ())),
                                 preferred_element_type=F32)
            new.append((m_new, l, alpha * acc + pv))
        return tuple(new)

    def step(n, slot, state):
        tops, carry = state
        tops_next = scores(n, 1 - slot, None)
        return tops_next, absorb(jnp.where(n == 0, i, n - 1), slot, tops, carry)

    def pair(n2, state):
        return step(2 * n2 + 1, 1, step(2 * n2, 0, state))

    def tail_even(state):
        tops, carry = state
        return absorb(jnp.maximum(i - 1, 0), 0, tops, carry)

    def tail_odd(state):
        tops, carry = step(i - 1, 0, state)
        return absorb(i - 1, 1, tops, carry)

    init = (jnp.full((1, t), NEG_BIG, F32), jnp.zeros((1, t), F32), jnp.zeros((LANES, t), F32))
    state = lax.fori_loop(0, i // 2, pair, (scores(i, 0, causal), (init, init)))
    carry = lax.cond(i % 2 == 1, tail_odd, tail_even, state)
    (_, l1, acc1), (_, l2, acc2) = carry
    o = acc1 / l1 - lam_ref[...] * (acc2 / l2)
    o = o * lax.rsqrt(jnp.mean(o * o, axis=0, keepdims=True) + EPS)
    o_ref[0] = (o.T * (g_ref[...] * out_scale)).astype(BF16)


def _top2_route(logits):
    lane = lax.broadcasted_iota(jnp.int32, logits.shape, 1)
    lg = jnp.where(lane < N_EXPERTS, logits, -jnp.inf)
    m1 = jnp.max(lg, axis=-1, keepdims=True)
    i1 = jnp.min(jnp.where(lg == m1, lane, LANES), axis=-1, keepdims=True)
    lg2 = jnp.where(lane == i1, -jnp.inf, lg)
    m2 = jnp.max(lg2, axis=-1, keepdims=True)
    i2 = jnp.min(jnp.where(lg2 == m2, lane, LANES), axis=-1, keepdims=True)
    e = jnp.exp(m2 - m1)
    w1 = 1.0 / (1.0 + e)
    fields = (i1.astype(F32), i2.astype(F32), w1, e * w1)
    out = jnp.zeros(logits.shape, F32)
    for n, val in enumerate(fields):
        out = jnp.where(lane == n, val, out)
    return out


def _out_proj_kernel(*refs, moe):
    if moe:
        mix_ref, qm_ref, km_ref, vm_ref, w_ref, x_ref, g_ref, wr_ref, x_out, h_out, c_out = refs
    else:
        mix_ref, qm_ref, km_ref, vm_ref, w_ref, x_ref, g_ref, x_out, h_out = refs
    mix_width = mix_ref.shape[-1]
    o_mem = _mem_attn(qm_ref[...], km_ref[...], vm_ref[...])
    x = x_ref[...] + _dot(mix_ref[...], w_ref[:mix_width, :]) + _dot(o_mem, w_ref[mix_width:, :])
    x_out[...] = x
    h = _rms(x, g_ref[...])
    h_out[...] = h.astype(h_out.dtype)
    if moe:
        h_hi, h_lo = _split_bf16(h)
        w_hi, w_lo = _split_bf16(wr_ref[...])
        c_out[...] = _top2_route(_dot(h_hi, w_hi) + _dot(h_hi, w_lo) + _dot(h_lo, w_hi))


def _out_proj(o_mix, qm, km, vm, w, layer, x2d, g, w_router):
    rows, d_model = x2d.shape
    tm = ROW_TILE
    seq = rows // km.shape[1]
    row_spec = lambda n: pl.BlockSpec((tm, n), lambda i: (i, 0))
    const_spec = lambda a: pl.BlockSpec(a.shape, lambda i: (0, 0))
    mem_spec = pl.BlockSpec((None, None) + km.shape[2:], lambda i: (layer, i * tm // seq, 0, 0))
    moe = w_router is not None
    in_specs = [row_spec(o_mix.shape[1]), row_spec(qm.shape[1]), mem_spec, mem_spec,
                _layer_spec(w, layer), row_spec(d_model), const_spec(g)]
    args = [o_mix, qm, km, vm, w, x2d, g]
    out_shape = [jax.ShapeDtypeStruct((rows, d_model), F32),
                 jax.ShapeDtypeStruct((rows, d_model), F32 if moe else BF16)]
    out_specs = [row_spec(d_model), row_spec(d_model)]
    if moe:
        in_specs.append(const_spec(w_router))
        args.append(w_router)
        out_shape.append(jax.ShapeDtypeStruct((rows, LANES), F32))
        out_specs.append(row_spec(LANES))
    return pl.pallas_call(
        functools.partial(_out_proj_kernel, moe=moe),
        out_shape=tuple(out_shape),
        grid=(rows // tm,),
        in_specs=in_specs,
        out_specs=tuple(out_specs),
        compiler_params=_params("parallel"),
        name="out_proj_moe" if moe else "out_proj",
    )(*args)


def _swiglu_chunk(h, wg, wu, wd):
    g = _dot(h, wg)
    u = _dot(h, wu)
    return _dot((g * jax.nn.sigmoid(g) * u).astype(BF16), wd)


def _dense_ffn_kernel(h_ref, wgu_ref, wd_ref, x_ref, o_ref):
    d_ff = wd_ref.shape[0]
    o_ref[...] = x_ref[...] + _swiglu_chunk(h_ref[...], wgu_ref[:, :d_ff], wgu_ref[:, d_ff:],
                                            wd_ref[...])


def _dense_ffn(h, w_gate_up, w_down, layer, x2d):
    rows, d_model = x2d.shape
    tm = FFN_ROW_TILE
    resident = lambda a: _layer_spec(a, layer, pipeline_mode=pl.Buffered(1))
    row_spec = pl.BlockSpec((tm, d_model), lambda i: (i, 0))
    return pl.pallas_call(
        _dense_ffn_kernel,
        out_shape=jax.ShapeDtypeStruct((rows, d_model), F32),
        grid=(rows // tm,),
        in_specs=[row_spec, resident(w_gate_up), resident(w_down), row_spec],
        out_specs=row_spec,
        compiler_params=_params("parallel"),
        name="dense_ffn",
    )(h, w_gate_up, w_down, x2d)


def _route_plan(route, tm):
    n_tokens = route.shape[0]
    expert = jnp.concatenate([route[:, 0], route[:, 1]]).astype(jnp.int32)
    onehot = (expert[:, None] == jnp.arange(N_EXPERTS, dtype=jnp.int32)[None, :]).astype(jnp.int32)
    csum = jnp.cumsum(onehot, axis=0)
    tiles = (csum[-1] + tm - 1) // tm
    tile_end = jnp.cumsum(tiles)
    start = (tile_end - tiles) * tm
    pos = jnp.sum(onehot * (start[None, :] + csum - 1), axis=1)
    n_tiles = 2 * n_tokens // tm + N_EXPERTS
    tile_expert = jnp.sum(jnp.arange(n_tiles, dtype=jnp.int32)[:, None] >= tile_end[None, :], axis=1)
    n_used = tile_end[-1]
    tile_expert = jnp.minimum(tile_expert, tile_expert[n_used - 1])
    pos = pos.astype(jnp.int32)
    token = jnp.arange(2 * n_tokens, dtype=jnp.int32) % n_tokens
    src = jnp.zeros((n_tiles * tm,), jnp.int32).at[pos].set(token, unique_indices=True)
    return pos, src, jnp.concatenate([tile_expert, n_used[None]]).astype(jnp.int32)


def _for_each_row(n_rows, fn):
    def trip(g, c):
        for u in range(DMA_UNROLL):
            fn(g, u)
        return c

    lax.fori_loop(0, n_rows // DMA_UNROLL, trip, 0)


def _expert_ffn_kernel(plan_ref, src_ref, nxt_ref, h_ref, wg_ref, wu_ref, wd_ref, ys_ref,
                       hs_ref, hb_ref, acc_ref, sem, *, nf):
    i, f = pl.program_id(0), pl.program_id(1)
    n_tiles = pl.num_programs(0)
    n_used = plan_ref[n_tiles]
    used = i < n_used
    fetched = (i == 0) | (i - 1 < n_used)
    tm, d_model = hb_ref.shape
    share = tm // nf // DMA_UNROLL
    slot = i % 2

    def row_copy(idx_ref, g, u, s):
        token = idx_ref[0, 0, g * DMA_UNROLL + u]
        return pltpu.make_async_copy(h_ref.at[pl.ds(token, 1)],
                                     hs_ref.at[s, g, pl.ds(u, 1)], sem.at[s])

    @pl.when((i == 0) & (f == 0))
    def _():
        _for_each_row(tm, lambda g, u: row_copy(src_ref, g, u, 0).start())

    @pl.when(fetched & (f == 0))
    def _():
        _for_each_row(tm, lambda g, u: row_copy(src_ref, g, u, slot).wait())

    @pl.when(used & (f == 0))
    def _():
        hb_ref[...] = hs_ref[slot].reshape(tm, d_model).astype(BF16)
        acc_ref[...] = jnp.zeros_like(acc_ref)

    @pl.when(used)
    def _():
        for g in range(share):
            for u in range(DMA_UNROLL):
                row_copy(nxt_ref, f * share + g, u, 1 - slot).start()
        acc_ref[...] += _swiglu_chunk(hb_ref[...], wg_ref[0], wu_ref[0], wd_ref[0])

    @pl.when(used & (f == nf - 1))
    def _():
        ys_ref[...] = acc_ref[...]

    @pl.when(used & (i == n_tiles - 1) & (f == nf - 1))
    def _():
        _for_each_row(tm, lambda g, u: row_copy(nxt_ref, g, u, 1 - slot).wait())

    @pl.when(jnp.logical_not(used) & (f == nf - 1))
    def _():
        ys_ref[...] = jnp.zeros_like(ys_ref)


def _expert_ffn(h, src, plan, w_gate_up, w_down, layer, *, ff_tile):
    d_model = h.shape[1]
    d_ff = w_down.shape[2]
    nf = d_ff // ff_tile
    tm = MOE_ROW_TILE
    n_tiles = plan.shape[0] - 1
    n_rows = n_tiles * tm

    def chunk_of(i, f, plan_ref):
        return jnp.where(i < plan_ref[n_tiles], f, nf - 1)

    grid_spec = pltpu.PrefetchScalarGridSpec(
        num_scalar_prefetch=1,
        grid=(n_tiles, nf),
        in_specs=[
            pl.BlockSpec((1, 1, tm), lambda i, f, p: (i, 0, 0), memory_space=pltpu.SMEM),
            pl.BlockSpec((1, 1, tm), lambda i, f, p: (jnp.minimum(i + 1, n_tiles - 1), 0, 0),
                         memory_space=pltpu.SMEM),
            pl.BlockSpec(memory_space=pl.ANY),
            pl.BlockSpec((None, 1, d_model, ff_tile),
                         lambda i, f, p: (layer, p[i], 0, chunk_of(i, f, p))),
            pl.BlockSpec((None, 1, d_model, ff_tile),
                         lambda i, f, p: (layer, p[i], 0, chunk_of(i, f, p) + nf)),
            pl.BlockSpec((None, 1, ff_tile, d_model),
                         lambda i, f, p: (layer, p[i], chunk_of(i, f, p), 0)),
        ],
        out_specs=pl.BlockSpec((tm, d_model), lambda i, f, p: (i, 0)),
        scratch_shapes=[pltpu.VMEM((2, tm // DMA_UNROLL, DMA_UNROLL, d_model), F32),
                        pltpu.VMEM((tm, d_model), BF16),
                        pltpu.VMEM((tm, d_model), F32), pltpu.SemaphoreType.DMA((2,))],
    )
    src3 = src.reshape(n_tiles, 1, tm)
    return pl.pallas_call(
        functools.partial(_expert_ffn_kernel, nf=nf),
        out_shape=jax.ShapeDtypeStruct((n_rows, d_model), F32),
        grid_spec=grid_spec,
        compiler_params=_params("arbitrary", "arbitrary"),
        name="moe_experts",
    )(plan, src3, src3, h, w_gate_up, w_gate_up, w_down)


def _combine_kernel(p1_ref, p2_ref, ys_ref, route_ref, x_ref, o_ref, buf_ref, sem):
    chunk, d_model = x_ref.shape

    def row_copy(k, pos_ref, g, u):
        row = pos_ref[0, 0, g * DMA_UNROLL + u]
        return pltpu.make_async_copy(ys_ref.at[pl.ds(row, 1)],
                                     buf_ref.at[k, g, pl.ds(u, 1)], sem.at[k])

    def start(g, u):
        row_copy(0, p1_ref, g, u).start()
        row_copy(1, p2_ref, g, u).start()

    def wait(g, u):
        row_copy(0, p1_ref, g, u).wait()
        row_copy(1, p2_ref, g, u).wait()

    _for_each_row(chunk, start)
    _for_each_row(chunk, wait)
    route = route_ref[...]
    y1 = buf_ref[0].reshape(chunk, d_model)
    y2 = buf_ref[1].reshape(chunk, d_model)
    o_ref[...] = x_ref[...] + route[:, 2:3] * y1 + route[:, 3:4] * y2


def _combine(ys, pos, route, x2d):
    rows, d_model = x2d.shape
    chunk = ROUTE_CHUNK
    n_chunks = rows // chunk
    pos3 = pos.reshape(2 * n_chunks, 1, chunk)
    smem_spec = lambda off: pl.BlockSpec((1, 1, chunk), lambda c: (c + off, 0, 0), memory_space=pltpu.SMEM)
    row_spec = lambda n: pl.BlockSpec((chunk, n), lambda c: (c, 0))
    return pl.pallas_call(
        _combine_kernel,
        out_shape=jax.ShapeDtypeStruct((rows, d_model), F32),
        grid=(n_chunks,),
        in_specs=[smem_spec(0), smem_spec(n_chunks), pl.BlockSpec(memory_space=pl.ANY),
                  row_spec(LANES), row_spec(d_model)],
        out_specs=row_spec(d_model),
        scratch_shapes=[pltpu.VMEM((2, chunk // DMA_UNROLL, DMA_UNROLL, d_model), F32),
                        pltpu.SemaphoreType.DMA((2,))],
        compiler_params=_params("arbitrary"),
        name="moe_combine",
    )(pos3, pos3, ys, route, x2d)


def _moe_ffn(h, route, w_gate_up, w_down, layer, x2d, *, ff_tile):
    pos, src, plan = _route_plan(route, MOE_ROW_TILE)
    ys = _expert_ffn(h, src, plan, w_gate_up, w_down, layer, ff_tile=ff_tile)
    return _combine(ys, pos, route, x2d)


def _rope_tables(positions):
    half = ROPE_DIM // 2
    inv_freq = ROPE_THETA ** (-jnp.arange(0, ROPE_DIM, 2, dtype=F32) / ROPE_DIM)
    ang = positions.astype(F32).reshape(-1, 1) * inv_freq
    cos_sin = jnp.concatenate([jnp.cos(ang), jnp.sin(ang)], axis=-1)
    spread = np.zeros((ROPE_DIM, 3 * LANES), np.float32)
    offset = np.zeros((1, 3 * LANES), np.float32)
    for lane in range(LANES):
        m = lane % HEAD_DIM
        if m < ROPE_DIM:
            spread[m % half, lane] = 1.0
        else:
            offset[0, lane] = 1.0
        if m < half:
            spread[half + m, LANES + lane] = -1.0
        elif m < ROPE_DIM:
            spread[half + m - half, 2 * LANES + lane] = 1.0
    return cos_sin, jnp.asarray(spread), jnp.asarray(offset)


def _tile_gain(g, width):
    return jnp.tile(g.astype(F32), width // g.shape[-1]).reshape(1, width)


def _ff_tile(d_ff, limit):
    for step in (MXU_WIDTH, LANES):
        fits = [t for t in range(step, limit + 1, step) if d_ff % t == 0 and t > step]
        if fits:
            return max(fits)
    return LANES


def kernel(x, mem, positions, attn_norm, w_in, w_out, mem_norm, w_mem_kv, mem_q_norm, mem_k_norm,
           diff_q_norm, diff_k_norm, diff_lambda, diff_subln, ffn_norm, dense_w_gate_up,
           dense_w_down, w_router, moe_w_gate_up, moe_w_down):
    b, s, d_model = x.shape
    depth = w_in.shape[0]
    mem_len = mem.shape[1]
    mem_width = w_mem_kv.shape[-1] // 2
    mix_width = (w_in.shape[-1] - mem_width) // 3
    rows = b * s
    assert s % max(ROW_TILE, FFN_ROW_TILE, SB_TILE, DIFF_TILE) == 0 and mix_width % (2 * LANES) == 0

    row = lambda a: a.astype(F32).reshape(1, -1)
    k_gain = jnp.stack([_tile_gain(mem_k_norm[i], mem_width) for i in range(depth)])
    km, vm = _mem_kv(mem.reshape(b * mem_len, d_model), row(mem_norm), w_mem_kv.astype(BF16), k_gain)
    km = km.reshape(depth, b, mem_len, mem_width)
    vm = vm.reshape(depth, b, mem_len, mem_width)
    rope_tables = _rope_tables(positions)
    w_in, w_out, dense_w_gate_up, dense_w_down, moe_w_gate_up, moe_w_down = (
        w.astype(BF16) for w in (w_in, w_out, dense_w_gate_up, dense_w_down, moe_w_gate_up, moe_w_down))

    x2d = x.reshape(rows, d_model)
    for i in range(depth):
        j = i // 2
        is_diff = i % 2 == 1
        diff_args = None
        if is_diff:
            diff_args = (_tile_gain(diff_q_norm[j], LANES), _tile_gain(diff_k_norm[j], LANES),
                         *rope_tables)
        q, k, v, qm = _in_proj(x2d, row(attn_norm[i]), w_in, i,
                               _tile_gain(mem_q_norm[i], LANES), diff_args, mix_width=mix_width)
        to3 = lambda a: a.reshape(b, s, a.shape[-1])
        if is_diff:
            lam_init = 0.8 - 0.6 * math.exp(-0.3 * i)
            lp = diff_lambda[j].astype(F32)
            lam = jnp.exp(jnp.sum(lp[0] * lp[1])) - jnp.exp(jnp.sum(lp[2] * lp[3])) + lam_init
            o_mix = _token_attn(functools.partial(_diff_kernel, out_scale=1.0 - lam_init), "diff_attn",
                                DIFF_TILE, to3(q), to3(k), to3(v),
                                extra=(lam.reshape(1, 1), row(diff_subln[j])),
                                scratch=(pltpu.VMEM((2, DIFF_TILE, DIFF_TILE), F32),) * 2)
        else:
            o_mix = _token_attn(_sb_kernel, "sb_attn", SB_TILE, to3(q), to3(k), to3(v))
        router = None
        if is_diff:
            router = jnp.pad(w_router[j].astype(F32), ((0, 0), (0, LANES - w_router.shape[-1])))
        outs = _out_proj(o_mix.reshape(rows, mix_width), qm, km, vm, w_out, i, x2d,
                         row(ffn_norm[i]), router)
        if is_diff:
            x2d, h, route = outs
            x2d = _moe_ffn(h, route, moe_w_gate_up, moe_w_down, j, x2d,
                           ff_tile=_ff_tile(moe_w_down.shape[-2], 2048))
        else:
            x2d, h = outs
            x2d = _dense_ffn(h, dense_w_gate_up, dense_w_down, j, x2d)
    return x2d.reshape(b, s, d_model)
```

```python
import functools
import math

import jax
import jax.numpy as jnp
import numpy as np
from jax import lax
from jax.experimental import pallas as pl
from jax.experimental.pallas import tpu as pltpu

F32 = jnp.float32
BF16 = jnp.bfloat16

HEAD_DIM = 64
LANES = 128
MXU_WIDTH = 256
N_MEM_HEADS = 4
ROPE_DIM = HEAD_DIM // 4
ROPE_THETA = 500000.0
N_EXPERTS = 8
EPS = 1e-6
NEG_BIG = -1e30
SCALE = HEAD_DIM ** -0.5
LOG2E = math.log2(math.e)
SP_CLAMP = 126.0
SB_DONE = 104.0 * LOG2E
VMEM_LIMIT = 48 * 1024 * 1024

ROW_TILE = 512
FFN_ROW_TILE = 512
SB_TILE = 256
SB_GROUP = 4
DIFF_TILE = 1024
MOE_ROW_TILE = 512
ROUTE_CHUNK = 512
DMA_UNROLL = 8


def _params(*sem):
    return pltpu.CompilerParams(dimension_semantics=sem, vmem_limit_bytes=VMEM_LIMIT)


def _layer_spec(stacked, layer, **kwargs):
    return pl.BlockSpec((None,) + stacked.shape[1:], lambda i: (layer, 0, 0), **kwargs)


def _split_bf16(x):
    hi = x.astype(BF16)
    lo = (x - hi.astype(F32)).astype(BF16)
    return hi, lo


def _dot(a, b):
    return jnp.dot(a, b, preferred_element_type=F32)


def _dot_nt(a, b):
    return lax.dot_general(a, b, (((1,), (1,)), ((), ())), preferred_element_type=F32)


def _dot_split(x, m):
    hi, lo = _split_bf16(x)
    return _dot(hi, m) + _dot(lo, m)


def _group_ones(n, group):
    r = lax.broadcasted_iota(jnp.int32, (n, n), 0) // group
    c = lax.broadcasted_iota(jnp.int32, (n, n), 1) // group
    return (r == c).astype(BF16)


def _head_rms(t, gain, ones):
    ss = _dot_split(t * t, ones)
    return t * lax.rsqrt(ss * (1.0 / HEAD_DIM) + EPS) * gain


def _rms(x, g):
    return x * lax.rsqrt(jnp.mean(x * x, axis=-1, keepdims=True) + EPS) * g


def _mem_kv_kernel(mem_ref, g_ref, w_ref, kg_ref, k_out, v_out):
    width = k_out.shape[-1]
    mem_n = _rms(mem_ref[...], g_ref[...]).astype(BF16)
    kv = _dot(mem_n, w_ref[0])
    ones = _group_ones(width, HEAD_DIM)
    k_out[0] = _head_rms(kv[:, :width], kg_ref[0], ones).astype(BF16)
    v_out[0] = kv[:, width:].astype(BF16)


def _mem_kv(mem2d, mem_norm, w_mem_kv, k_gain):
    depth, d_model, two_w = w_mem_kv.shape
    width = two_w // 2
    rows = mem2d.shape[0]
    out = jax.ShapeDtypeStruct((depth, rows, width), BF16)
    return pl.pallas_call(
        _mem_kv_kernel,
        out_shape=(out, out),
        grid=(depth,),
        in_specs=[
            pl.BlockSpec((rows, d_model), lambda i: (0, 0)),
            pl.BlockSpec((1, d_model), lambda i: (0, 0)),
            pl.BlockSpec((1, d_model, two_w), lambda i: (i, 0, 0)),
            pl.BlockSpec((1, 1, width), lambda i: (i, 0, 0)),
        ],
        out_specs=(pl.BlockSpec((1, rows, width), lambda i: (i, 0, 0)),
                   pl.BlockSpec((1, rows, width), lambda i: (i, 0, 0))),
        compiler_params=_params("arbitrary"),
        name="mem_kv",
    )(mem2d, mem_norm, w_mem_kv, k_gain)


def _rope(t, cos, sin_lo, sin_hi):
    n = t.shape[-1]
    half = ROPE_DIM // 2
    return t * cos + pltpu.roll(t, n - half, 1) * sin_lo + pltpu.roll(t, half, 1) * sin_hi


def _in_proj_kernel(*refs, mix_width, diff):
    if diff:
        (x_ref, g_ref, w_ref, mg_ref, qg_ref, kg_ref, cs_ref, spread_ref, offset_ref,
         q_out, k_out, v_out, m_out) = refs
        hi = cs_ref[...].astype(BF16)
        mid, lo = _split_bf16(cs_ref[...] - hi.astype(F32))
        spread = spread_ref[...].astype(BF16)
        tables = _dot(hi, spread) + _dot(mid, spread) + _dot(lo, spread) + offset_ref[...]
        rope = functools.partial(_rope, cos=tables[:, :LANES], sin_lo=tables[:, LANES:2 * LANES],
                                 sin_hi=tables[:, 2 * LANES:])
    else:
        x_ref, g_ref, w_ref, mg_ref, q_out, k_out, v_out, m_out = refs
    h = _rms(x_ref[...], g_ref[...]).astype(BF16)
    ones = _group_ones(LANES, HEAD_DIM)
    chunk = 2 * LANES
    for c in range(mix_width // chunk):
        lo = c * chunk
        q = _dot(h, w_ref[:, lo:lo + chunk])
        k = _dot(h, w_ref[:, mix_width + lo:mix_width + lo + chunk])
        for half in range(2):
            sl = slice(half * LANES, (half + 1) * LANES)
            dst = slice(lo + half * LANES, lo + (half + 1) * LANES)
            qh, kh = q[:, sl], k[:, sl]
            if diff:
                qh = rope(_head_rms(qh, qg_ref[...], ones))
                kh = rope(_head_rms(kh, kg_ref[...], ones))
            q_out[:, dst] = (qh * (SCALE * LOG2E)).astype(BF16)
            k_out[:, dst] = kh.astype(BF16)
    v_out[...] = _dot(h, w_ref[:, 2 * mix_width:3 * mix_width]).astype(BF16)
    qm = _dot(h, w_ref[:, 3 * mix_width:])
    mem_width = qm.shape[-1]
    for c in range(mem_width // LANES):
        sl = slice(c * LANES, (c + 1) * LANES)
        m_out[:, sl] = (_head_rms(qm[:, sl], mg_ref[...], ones) * SCALE).astype(BF16)


def _in_proj(x2d, g, w, layer, mem_q_gain, diff_args, *, mix_width):
    rows, d_model = x2d.shape
    in_width = w.shape[2]
    mem_width = in_width - 3 * mix_width
    tm = ROW_TILE
    row_spec = lambda n: pl.BlockSpec((tm, n), lambda i: (i, 0))
    const_spec = lambda a: pl.BlockSpec(a.shape, lambda i: (0, 0))
    diff = diff_args is not None
    in_specs = [row_spec(d_model), const_spec(g), _layer_spec(w, layer), const_spec(mem_q_gain)]
    args = [x2d, g, w, mem_q_gain]
    if diff:
        q_gain, k_gain, cos_sin, spread, offset = diff_args
        in_specs += [const_spec(q_gain), const_spec(k_gain), row_spec(cos_sin.shape[1]),
                     const_spec(spread), const_spec(offset)]
        args += [q_gain, k_gain, cos_sin, spread, offset]
    mix = jax.ShapeDtypeStruct((rows, mix_width), BF16)
    return pl.pallas_call(
        functools.partial(_in_proj_kernel, mix_width=mix_width, diff=diff),
        out_shape=(mix, mix, mix, jax.ShapeDtypeStruct((rows, mem_width), BF16)),
        grid=(rows // tm,),
        in_specs=in_specs,
        out_specs=(row_spec(mix_width), row_spec(mix_width), row_spec(mix_width),
                   row_spec(mem_width)),
        compiler_params=_params("parallel"),
        name="in_proj_diff" if diff else "in_proj_sb",
    )(*args)


def _mem_attn(q, k, v):
    lane = lax.broadcasted_iota(jnp.int32, (1, q.shape[-1]), 1) // HEAD_DIM
    out = jnp.zeros(q.shape, F32)
    for hd in range(N_MEM_HEADS):
        sel = lane == hd
        s = _dot_nt(jnp.where(sel, q, 0), k)
        p = jnp.exp(s - jnp.max(s, axis=-1, keepdims=True))
        p = p / jnp.sum(p, axis=-1, keepdims=True)
        out = out + _dot(p.astype(BF16), jnp.where(sel, v, 0))
    return out.astype(BF16)


def _sb_kernel(q_ref, k_ref, v_ref, o_ref):
    t = SB_TILE
    lane = lax.broadcasted_iota(jnp.int32, (1, LANES), 1)
    first = lane < HEAD_DIM
    key = lax.broadcasted_iota(jnp.int32, (t, t), 0)
    qry = lax.broadcasted_iota(jnp.int32, (t, t), 1)
    later = (qry > key).astype(BF16)
    strict = key < qry

    def block(j, q_heads, carry, mask, live):
        spent_a, spent_b, o = carry
        kb = k_ref[0, pl.ds(j * t, t), :]
        vb = v_ref[0, pl.ds(j * t, t), :]
        if live is not None:
            vb = jnp.where(live, vb, 0)
        v_heads = (jnp.where(first, vb, 0), jnp.where(first, 0, vb))
        spent = [spent_a, spent_b]
        for hd in range(2):
            z = _dot_nt(kb, q_heads[hd])
            if mask is not None:
                z = jnp.where(mask, z, NEG_BIG)
            sp = jnp.maximum(z, jnp.log2(1.0 + jnp.exp2(jnp.minimum(z, SP_CLAMP))))
            after = _dot(later, sp.astype(BF16)) + spent[hd]
            w = jnp.exp2(z - sp - after)
            o = o + lax.dot_general(v_heads[hd], w.astype(BF16), (((0,), (0,)), ((), ())),
                                    preferred_element_type=F32)
            spent[hd] = spent[hd] + jnp.sum(sp, axis=0, keepdims=True)
        return spent[0], spent[1], o

    def stick_left(spent_a, spent_b):
        return (jnp.min(jnp.minimum(spent_a, spent_b)) < SB_DONE).astype(jnp.int32)

    zero = jnp.zeros((1, t), F32)
    started = []
    for sub in range(q_ref.shape[1] // t):
        i = pl.program_id(2) * (q_ref.shape[1] // t) + sub
        q = q_ref[0, sub * t:(sub + 1) * t, :]
        q_heads = (jnp.where(first, q, 0), jnp.where(first, 0, q))
        carry = block(i, q_heads, (zero, zero, jnp.zeros((LANES, t), F32)), strict, None)
        carry = block(jnp.maximum(i - 1, 0), q_heads, carry, None, i > 0)
        started.append((i, q_heads, carry))

    for sub, (i, q_heads, carry) in enumerate(started):
        def earlier(state, i=i, q_heads=q_heads):
            n, _, carry = state
            carry = block(i - 1 - n, q_heads, carry, None, None)
            return n + 1, stick_left(carry[0], carry[1]), carry

        state = (jnp.int32(1), stick_left(carry[0], carry[1]), carry)
        state = lax.while_loop(lambda st, i=i: (st[0] < i) & (st[1] > 0), earlier, state)
        o_ref[0, sub * t:(sub + 1) * t, :] = state[2][2].T.astype(BF16)


def _token_attn(kernel, name, t, q, k, v, extra=(), scratch=()):
    b, s, width = q.shape
    q_spec = pl.BlockSpec((1, t, LANES), lambda bi, p, i: (bi, i, p))
    kv_spec = pl.BlockSpec((1, s, LANES), lambda bi, p, i: (bi, 0, p))
    extra_specs = [pl.BlockSpec(a.shape, lambda bi, p, i: (0, 0)) for a in extra]
    return pl.pallas_call(
        kernel,
        out_shape=jax.ShapeDtypeStruct((b, s, width), BF16),
        grid=(b, width // LANES, s // t),
        in_specs=[q_spec, kv_spec, kv_spec] + extra_specs,
        out_specs=q_spec,
        scratch_shapes=list(scratch),
        compiler_params=_params("parallel", "parallel", "arbitrary"),
        name=name,
    )(q, k, v, *extra)


def _diff_kernel(q_ref, k_ref, v_ref, lam_ref, g_ref, o_ref, *s_refs, out_scale):
    t = q_ref.shape[1]
    i = pl.program_id(2)
    q = q_ref[0]
    lane = lax.broadcasted_iota(jnp.int32, (1, LANES), 1)
    first = lane < HEAD_DIM
    q_maps = (jnp.where(first, q, 0), jnp.where(first, 0, q))
    key = lax.broadcasted_iota(jnp.int32, (t, t), 0)
    qry = lax.broadcasted_iota(jnp.int32, (t, t), 1)
    causal = key <= qry

    def scores(j, slot, mask):
        kb = k_ref[0, pl.ds(j * t, t), :]
        tops = []
        for c in range(2):
            s = _dot_nt(kb, q_maps[c])
            if mask is not None:
                s = jnp.where(mask, s, NEG_BIG)
            s_refs[slot][c] = s
            tops.append(jnp.max(s, axis=0, keepdims=True))
        return tuple(tops)

    def absorb(j, slot, tops, carry):
        vb = v_ref[0, pl.ds(j * t, t), :]
        new = []
        for c in range(2):
            m, l, acc = carry[c]
            m_new = jnp.maximum(m, tops[c])
            alpha = jnp.exp2(m - m_new)
            p = jnp.exp2(s_refs[slot][c] - m_new)
            l = alpha * l + jnp.sum(p, axis=0, keepdims=True)
            pv = lax.dot_general(vb, p.astype(BF16), (((0,), (0,)), ((), ())),
                                 preferred_element_type=F32)
            new.append((m_new, l, alpha * acc + pv))
        return tuple(new)

    def step(n, slot, state):
        tops, carry = state
        tops_next = scores(n, 1 - slot, None)
        return tops_next, absorb(jnp.where(n == 0, i, n - 1), slot, tops, carry)

    def pair(n2, state):
        return step(2 * n2 + 1, 1, step(2 * n2, 0, state))

    def tail_even(state):
        tops, carry = state
        return absorb(jnp.maximum(i - 1, 0), 0, tops, carry)

    def tail_odd(state):
        tops, carry = step(i - 1, 0, state)
        return absorb(i - 1, 1, tops, carry)

    init = (jnp.full((1, t), NEG_BIG, F32), jnp.zeros((1, t), F32), jnp.zeros((LANES, t), F32))
    state = lax.fori_loop(0, i // 2, pair, (scores(i, 0, causal), (init, init)))
    carry = lax.cond(i % 2 == 1, tail_odd, tail_even, state)
    (_, l1, acc1), (_, l2, acc2) = carry
    o = acc1 / l1 - lam_ref[...] * (acc2 / l2)
    o = o * lax.rsqrt(jnp.mean(o * o, axis=0, keepdims=True) + EPS)
    o_ref[0] = (o.T * (g_ref[...] * out_scale)).astype(BF16)


def _top2_route(logits):
    lane = lax.broadcasted_iota(jnp.int32, logits.shape, 1)
    lg = jnp.where(lane < N_EXPERTS, logits, -jnp.inf)
    m1 = jnp.max(lg, axis=-1, keepdims=True)
    i1 = jnp.min(jnp.where(lg == m1, lane, LANES), axis=-1, keepdims=True)
    lg2 = jnp.where(lane == i1, -jnp.inf, lg)
    m2 = jnp.max(lg2, axis=-1, keepdims=True)
    i2 = jnp.min(jnp.where(lg2 == m2, lane, LANES), axis=-1, keepdims=True)
    e = jnp.exp(m2 - m1)
    w1 = 1.0 / (1.0 + e)
    fields = (i1.astype(F32), i2.astype(F32), w1, e * w1)
    out = jnp.zeros(logits.shape, F32)
    for n, val in enumerate(fields):
        out = jnp.where(lane == n, val, out)
    return out


def _out_proj_kernel(*refs, moe):
    if moe:
        mix_ref, qm_ref, km_ref, vm_ref, w_ref, x_ref, g_ref, wr_ref, x_out, h_out, c_out = refs
    else:
        mix_ref, qm_ref, km_ref, vm_ref, w_ref, x_ref, g_ref, x_out, h_out = refs
    mix_width = mix_ref.shape[-1]
    o_mem = _mem_attn(qm_ref[...], km_ref[...], vm_ref[...])
    x = x_ref[...] + _dot(mix_ref[...], w_ref[:mix_width, :]) + _dot(o_mem, w_ref[mix_width:, :])
    x_out[...] = x
    h = _rms(x, g_ref[...])
    h_out[...] = h.astype(h_out.dtype)
    if moe:
        h_hi, h_lo = _split_bf16(h)
        w_hi, w_lo = _split_bf16(wr_ref[...])
        c_out[...] = _top2_route(_dot(h_hi, w_hi) + _dot(h_hi, w_lo) + _dot(h_lo, w_hi))


def _out_proj(o_mix, qm, km, vm, w, layer, x2d, g, w_router):
    rows, d_model = x2d.shape
    tm = ROW_TILE
    seq = rows // km.shape[1]
    row_spec = lambda n: pl.BlockSpec((tm, n), lambda i: (i, 0))
    const_spec = lambda a: pl.BlockSpec(a.shape, lambda i: (0, 0))
    mem_spec = pl.BlockSpec((None, None) + km.shape[2:], lambda i: (layer, i * tm // seq, 0, 0))
    moe = w_router is not None
    in_specs = [row_spec(o_mix.shape[1]), row_spec(qm.shape[1]), mem_spec, mem_spec,
                _layer_spec(w, layer), row_spec(d_model), const_spec(g)]
    args = [o_mix, qm, km, vm, w, x2d, g]
    out_shape = [jax.ShapeDtypeStruct((rows, d_model), F32),
                 jax.ShapeDtypeStruct((rows, d_model), F32 if moe else BF16)]
    out_specs = [row_spec(d_model), row_spec(d_model)]
    if moe:
        in_specs.append(const_spec(w_router))
        args.append(w_router)
        out_shape.append(jax.ShapeDtypeStruct((rows, LANES), F32))
        out_specs.append(row_spec(LANES))
    return pl.pallas_call(
        functools.partial(_out_proj_kernel, moe=moe),
        out_shape=tuple(out_shape),
        grid=(rows // tm,),
        in_specs=in_specs,
        out_specs=tuple(out_specs),
        compiler_params=_params("parallel"),
        name="out_proj_moe" if moe else "out_proj",
    )(*args)


def _swiglu_chunk(h, wg, wu, wd):
    g = _dot(h, wg)
    u = _dot(h, wu)
    return _dot((g * jax.nn.sigmoid(g) * u).astype(BF16), wd)


def _dense_ffn_kernel(h_ref, wgu_ref, wd_ref, x_ref, o_ref):
    d_ff = wd_ref.shape[0]
    o_ref[...] = x_ref[...] + _swiglu_chunk(h_ref[...], wgu_ref[:, :d_ff], wgu_ref[:, d_ff:],
                                            wd_ref[...])


def _dense_ffn(h, w_gate_up, w_down, layer, x2d):
    rows, d_model = x2d.shape
    tm = FFN_ROW_TILE
    resident = lambda a: _layer_spec(a, layer, pipeline_mode=pl.Buffered(1))
    row_spec = pl.BlockSpec((tm, d_model), lambda i: (i, 0))
    return pl.pallas_call(
        _dense_ffn_kernel,
        out_shape=jax.ShapeDtypeStruct((rows, d_model), F32),
        grid=(rows // tm,),
        in_specs=[row_spec, resident(w_gate_up), resident(w_down), row_spec],
        out_specs=row_spec,
        compiler_params=_params("parallel"),
        name="dense_ffn",
    )(h, w_gate_up, w_down, x2d)


def _route_plan(route, tm):
    n_tokens = route.shape[0]
    expert = jnp.concatenate([route[:, 0], route[:, 1]]).astype(jnp.int32)
    onehot = (expert[:, None] == jnp.arange(N_EXPERTS, dtype=jnp.int32)[None, :]).astype(jnp.int32)
    csum = jnp.cumsum(onehot, axis=0)
    tiles = (csum[-1] + tm - 1) // tm
    tile_end = jnp.cumsum(tiles)
    start = (tile_end - tiles) * tm
    pos = jnp.sum(onehot * (start[None, :] + csum - 1), axis=1)
    n_tiles = 2 * n_tokens // tm + N_EXPERTS
    tile_expert = jnp.sum(jnp.arange(n_tiles, dtype=jnp.int32)[:, None] >= tile_end[None, :], axis=1)
    n_used = tile_end[-1]
    tile_expert = jnp.minimum(tile_expert, tile_expert[n_used - 1])
    pos = pos.astype(jnp.int32)
    token = jnp.arange(2 * n_tokens, dtype=jnp.int32) % n_tokens
    src = jnp.zeros((n_tiles * tm,), jnp.int32).at[pos].set(token, unique_indices=True)
    return pos, src, jnp.concatenate([tile_expert, n_used[None]]).astype(jnp.int32)


def _for_each_row(n_rows, fn):
    def trip(g, c):
        for u in range(DMA_UNROLL):
            fn(g, u)
        return c

    lax.fori_loop(0, n_rows // DMA_UNROLL, trip, 0)


def _expert_ffn_kernel(plan_ref, src_ref, nxt_ref, h_ref, wg_ref, wu_ref, wd_ref, ys_ref,
                       hs_ref, hb_ref, acc_ref, sem, *, nf):
    i, f = pl.program_id(0), pl.program_id(1)
    n_tiles = pl.num_programs(0)
    n_used = plan_ref[n_tiles]
    used = i < n_used
    fetched = (i == 0) | (i - 1 < n_used)
    tm, d_model = hb_ref.shape
    share = tm // nf // DMA_UNROLL
    slot = i % 2

    def row_copy(idx_ref, g, u, s):
        token = idx_ref[0, 0, g * DMA_UNROLL + u]
        return pltpu.make_async_copy(h_ref.at[pl.ds(token, 1)],
                                     hs_ref.at[s, g, pl.ds(u, 1)], sem.at[s])

    @pl.when((i == 0) & (f == 0))
    def _():
        _for_each_row(tm, lambda g, u: row_copy(src_ref, g, u, 0).start())

    @pl.when(fetched & (f == 0))
    def _():
        _for_each_row(tm, lambda g, u: row_copy(src_ref, g, u, slot).wait())

    @pl.when(used & (f == 0))
    def _():
        hb_ref[...] = hs_ref[slot].reshape(tm, d_model).astype(BF16)
        acc_ref[...] = jnp.zeros_like(acc_ref)

    @pl.when(used)
    def _():
        for g in range(share):
            for u in range(DMA_UNROLL):
                row_copy(nxt_ref, f * share + g, u, 1 - slot).start()
        acc_ref[...] += _swiglu_chunk(hb_ref[...], wg_ref[0], wu_ref[0], wd_ref[0])

    @pl.when(used & (f == nf - 1))
    def _():
        ys_ref[...] = acc_ref[...]

    @pl.when(used & (i == n_tiles - 1) & (f == nf - 1))
    def _():
        _for_each_row(tm, lambda g, u: row_copy(nxt_ref, g, u, 1 - slot).wait())

    @pl.when(jnp.logical_not(used) & (f == nf - 1))
    def _():
        ys_ref[...] = jnp.zeros_like(ys_ref)


def _expert_ffn(h, src, plan, w_gate_up, w_down, layer, *, ff_tile):
    d_model = h.shape[1]
    d_ff = w_down.shape[2]
    nf = d_ff // ff_tile
    tm = MOE_ROW_TILE
    n_tiles = plan.shape[0] - 1
    n_rows = n_tiles * tm

    def chunk_of(i, f, plan_ref):
        return jnp.where(i < plan_ref[n_tiles], f, nf - 1)

    grid_spec = pltpu.PrefetchScalarGridSpec(
        num_scalar_prefetch=1,
        grid=(n_tiles, nf),
        in_specs=[
            pl.BlockSpec((1, 1, tm), lambda i, f, p: (i, 0, 0), memory_space=pltpu.SMEM),
            pl.BlockSpec((1, 1, tm), lambda i, f, p: (jnp.minimum(i + 1, n_tiles - 1), 0, 0),
                         memory_space=pltpu.SMEM),
            pl.BlockSpec(memory_space=pl.ANY),
            pl.BlockSpec((None, 1, d_model, ff_tile),
                         lambda i, f, p: (layer, p[i], 0, chunk_of(i, f, p))),
            pl.BlockSpec((None, 1, d_model, ff_tile),
                         lambda i, f, p: (layer, p[i], 0, chunk_of(i, f, p) + nf)),
            pl.BlockSpec((None, 1, ff_tile, d_model),
                         lambda i, f, p: (layer, p[i], chunk_of(i, f, p), 0)),
        ],
        out_specs=pl.BlockSpec((tm, d_model), lambda i, f, p: (i, 0)),
        scratch_shapes=[pltpu.VMEM((2, tm // DMA_UNROLL, DMA_UNROLL, d_model), F32),
                        pltpu.VMEM((tm, d_model), BF16),
                        pltpu.VMEM((tm, d_model), F32), pltpu.SemaphoreType.DMA((2,))],
    )
    src3 = src.reshape(n_tiles, 1, tm)
    return pl.pallas_call(
        functools.partial(_expert_ffn_kernel, nf=nf),
        out_shape=jax.ShapeDtypeStruct((n_rows, d_model), F32),
        grid_spec=grid_spec,
        compiler_params=_params("arbitrary", "arbitrary"),
        name="moe_experts",
    )(plan, src3, src3, h, w_gate_up, w_gate_up, w_down)


def _combine_kernel(p1_ref, p2_ref, ys_ref, route_ref, x_ref, o_ref, buf_ref, sem):
    chunk, d_model = x_ref.shape

    def row_copy(k, pos_ref, g, u):
        row = pos_ref[0, 0, g * DMA_UNROLL + u]
        return pltpu.make_async_copy(ys_ref.at[pl.ds(row, 1)],
                                     buf_ref.at[k, g, pl.ds(u, 1)], sem.at[k])

    def start(g, u):
        row_copy(0, p1_ref, g, u).start()
        row_copy(1, p2_ref, g, u).start()

    def wait(g, u):
        row_copy(0, p1_ref, g, u).wait()
        row_copy(1, p2_ref, g, u).wait()

    _for_each_row(chunk, start)
    _for_each_row(chunk, wait)
    route = route_ref[...]
    y1 = buf_ref[0].reshape(chunk, d_model)
    y2 = buf_ref[1].reshape(chunk, d_model)
    o_ref[...] = x_ref[...] + route[:, 2:3] * y1 + route[:, 3:4] * y2


def _combine(ys, pos, route, x2d):
    rows, d_model = x2d.shape
    chunk = ROUTE_CHUNK
    n_chunks = rows // chunk
    pos3 = pos.reshape(2 * n_chunks, 1, chunk)
    smem_spec = lambda off: pl.BlockSpec((1, 1, chunk), lambda c: (c + off, 0, 0), memory_space=pltpu.SMEM)
    row_spec = lambda n: pl.BlockSpec((chunk, n), lambda c: (c, 0))
    return pl.pallas_call(
        _combine_kernel,
        out_shape=jax.ShapeDtypeStruct((rows, d_model), F32),
        grid=(n_chunks,),
        in_specs=[smem_spec(0), smem_spec(n_chunks), pl.BlockSpec(memory_space=pl.ANY),
                  row_spec(LANES), row_spec(d_model)],
        out_specs=row_spec(d_model),
        scratch_shapes=[pltpu.VMEM((2, chunk // DMA_UNROLL, DMA_UNROLL, d_model), F32),
                        pltpu.SemaphoreType.DMA((2,))],
        compiler_params=_params("arbitrary"),
        name="moe_combine",
    )(pos3, pos3, ys, route, x2d)


def _moe_ffn(h, route, w_gate_up, w_down, layer, x2d, *, ff_tile):
    pos, src, plan = _route_plan(route, MOE_ROW_TILE)
    ys = _expert_ffn(h, src, plan, w_gate_up, w_down, layer, ff_tile=ff_tile)
    return _combine(ys, pos, route, x2d)


def _rope_tables(positions):
    half = ROPE_DIM // 2
    inv_freq = ROPE_THETA ** (-jnp.arange(0, ROPE_DIM, 2, dtype=F32) / ROPE_DIM)
    ang = positions.astype(F32).reshape(-1, 1) * inv_freq
    cos_sin = jnp.concatenate([jnp.cos(ang), jnp.sin(ang)], axis=-1)
    spread = np.zeros((ROPE_DIM, 3 * LANES), np.float32)
    offset = np.zeros((1, 3 * LANES), np.float32)
    for lane in range(LANES):
        m = lane % HEAD_DIM
        if m < ROPE_DIM:
            spread[m % half, lane] = 1.0
        else:
            offset[0, lane] = 1.0
        if m < half:
            spread[half + m, LANES + lane] = -1.0
        elif m < ROPE_DIM:
            spread[half + m - half, 2 * LANES + lane] = 1.0
    return cos_sin, jnp.asarray(spread), jnp.asarray(offset)


def _tile_gain(g, width):
    return jnp.tile(g.astype(F32), width // g.shape[-1]).reshape(1, width)


def _ff_tile(d_ff, limit):
    for step in (MXU_WIDTH, LANES):
        fits = [t for t in range(step, limit + 1, step) if d_ff % t == 0 and t > step]
        if fits:
            return max(fits)
    return LANES


def kernel(x, mem, positions, attn_norm, w_in, w_out, mem_norm, w_mem_kv, mem_q_norm, mem_k_norm,
           diff_q_norm, diff_k_norm, diff_lambda, diff_subln, ffn_norm, dense_w_gate_up,
           dense_w_down, w_router, moe_w_gate_up, moe_w_down):
    b, s, d_model = x.shape
    depth = w_in.shape[0]
    mem_len = mem.shape[1]
    mem_width = w_mem_kv.shape[-1] // 2
    mix_width = (w_in.shape[-1] - mem_width) // 3
    rows = b * s
    assert (s % max(ROW_TILE, FFN_ROW_TILE, SB_GROUP * SB_TILE, DIFF_TILE) == 0
            and mix_width % (2 * LANES) == 0)

    row = lambda a: a.astype(F32).reshape(1, -1)
    k_gain = jnp.stack([_tile_gain(mem_k_norm[i], mem_width) for i in range(depth)])
    km, vm = _mem_kv(mem.reshape(b * mem_len, d_model), row(mem_norm), w_mem_kv.astype(BF16), k_gain)
    km = km.reshape(depth, b, mem_len, mem_width)
    vm = vm.reshape(depth, b, mem_len, mem_width)
    rope_tables = _rope_tables(positions)
    w_in, w_out, dense_w_gate_up, dense_w_down, moe_w_gate_up, moe_w_down = (
        w.astype(BF16) for w in (w_in, w_out, dense_w_gate_up, dense_w_down, moe_w_gate_up, moe_w_down))

    x2d = x.reshape(rows, d_model)
    for i in range(depth):
        j = i // 2
        is_diff = i % 2 == 1
        diff_args = None
        if is_diff:
            diff_args = (_tile_gain(diff_q_norm[j], LANES), _tile_gain(diff_k_norm[j], LANES),
                         *rope_tables)
        q, k, v, qm = _in_proj(x2d, row(attn_norm[i]), w_in, i,
                               _tile_gain(mem_q_norm[i], LANES), diff_args, mix_width=mix_width)
        to3 = lambda a: a.reshape(b, s, a.shape[-1])
        if is_diff:
            lam_init = 0.8 - 0.6 * math.exp(-0.3 * i)
            lp = diff_lambda[j].astype(F32)
            lam = jnp.exp(jnp.sum(lp[0] * lp[1])) - jnp.exp(jnp.sum(lp[2] * lp[3])) + lam_init
            o_mix = _token_attn(functools.partial(_diff_kernel, out_scale=1.0 - lam_init), "diff_attn",
                                DIFF_TILE, to3(q), to3(k), to3(v),
                                extra=(lam.reshape(1, 1), row(diff_subln[j])),
                                scratch=(pltpu.VMEM((2, DIFF_TILE, DIFF_TILE), F32),) * 2)
        else:
            o_mix = _token_attn(_sb_kernel, "sb_attn", SB_GROUP * SB_TILE, to3(q), to3(k), to3(v))
        router = None
        if is_diff:
            router = jnp.pad(w_router[j].astype(F32), ((0, 0), (0, LANES - w_router.shape[-1])))
        outs = _out_proj(o_mix.reshape(rows, mix_width), qm, km, vm, w_out, i, x2d,
                         row(ffn_norm[i]), router)
        if is_diff:
            x2d, h, route = outs
            x2d = _moe_ffn(h, route, moe_w_gate_up, moe_w_down, j, x2d,
                           ff_tile=_ff_tile(moe_w_down.shape[-2], 2048))
        else:
            x2d, h = outs
            x2d = _dense_ffn(h, dense_w_gate_up, dense_w_down, j, x2d)
    return x2d.reshape(b, s, d_model)
```

```python
import functools
import math

import jax
import jax.numpy as jnp
import numpy as np
from jax import lax
from jax.experimental import pallas as pl
from jax.experimental.pallas import tpu as pltpu

F32 = jnp.float32
BF16 = jnp.bfloat16

HEAD_DIM = 64
LANES = 128
MXU_WIDTH = 256
N_MEM_HEADS = 4
ROPE_DIM = HEAD_DIM // 4
ROPE_THETA = 500000.0
N_EXPERTS = 8
EPS = 1e-6
NEG_BIG = -1e30
SCALE = HEAD_DIM ** -0.5
LOG2E = math.log2(math.e)
SP_CLAMP = 126.0
SB_DONE = 104.0 * LOG2E
VMEM_LIMIT = 48 * 1024 * 1024

ROW_TILE = 1024
FFN_ROW_TILE = 512
SB_TILE = 256
SB_GROUP = 4
DIFF_TILE = 1024
MOE_ROW_TILE = 512
ROUTE_CHUNK = 512
DMA_UNROLL = 8


def _params(*sem):
    return pltpu.CompilerParams(dimension_semantics=sem, vmem_limit_bytes=VMEM_LIMIT)


def _layer_spec(stacked, layer, **kwargs):
    return pl.BlockSpec((None,) + stacked.shape[1:], lambda i: (layer, 0, 0), **kwargs)


def _split_bf16(x):
    hi = x.astype(BF16)
    lo = (x - hi.astype(F32)).astype(BF16)
    return hi, lo


def _dot(a, b):
    return jnp.dot(a, b, preferred_element_type=F32)


def _dot_nt(a, b):
    return lax.dot_general(a, b, (((1,), (1,)), ((), ())), preferred_element_type=F32)


def _dot_split(x, m):
    hi, lo = _split_bf16(x)
    return _dot(hi, m) + _dot(lo, m)


def _group_ones(n, group):
    r = lax.broadcasted_iota(jnp.int32, (n, n), 0) // group
    c = lax.broadcasted_iota(jnp.int32, (n, n), 1) // group
    return (r == c).astype(BF16)


def _head_rms(t, gain, ones):
    ss = _dot_split(t * t, ones)
    return t * lax.rsqrt(ss * (1.0 / HEAD_DIM) + EPS) * gain


def _rms(x, g):
    return x * lax.rsqrt(jnp.mean(x * x, axis=-1, keepdims=True) + EPS) * g


def _mem_kv_kernel(mem_ref, g_ref, w_ref, kg_ref, k_out, v_out):
    width = k_out.shape[-1]
    mem_n = _rms(mem_ref[...], g_ref[...]).astype(BF16)
    kv = _dot(mem_n, w_ref[0])
    ones = _group_ones(width, HEAD_DIM)
    k_out[0] = _head_rms(kv[:, :width], kg_ref[0], ones).astype(BF16)
    v_out[0] = kv[:, width:].astype(BF16)


def _mem_kv(mem2d, mem_norm, w_mem_kv, k_gain):
    depth, d_model, two_w = w_mem_kv.shape
    width = two_w // 2
    rows = mem2d.shape[0]
    out = jax.ShapeDtypeStruct((depth, rows, width), BF16)
    return pl.pallas_call(
        _mem_kv_kernel,
        out_shape=(out, out),
        grid=(depth,),
        in_specs=[
            pl.BlockSpec((rows, d_model), lambda i: (0, 0)),
            pl.BlockSpec((1, d_model), lambda i: (0, 0)),
            pl.BlockSpec((1, d_model, two_w), lambda i: (i, 0, 0)),
            pl.BlockSpec((1, 1, width), lambda i: (i, 0, 0)),
        ],
        out_specs=(pl.BlockSpec((1, rows, width), lambda i: (i, 0, 0)),
                   pl.BlockSpec((1, rows, width), lambda i: (i, 0, 0))),
        compiler_params=_params("arbitrary"),
        name="mem_kv",
    )(mem2d, mem_norm, w_mem_kv, k_gain)


def _rope(t, cos, sin_lo, sin_hi):
    n = t.shape[-1]
    half = ROPE_DIM // 2
    return t * cos + pltpu.roll(t, n - half, 1) * sin_lo + pltpu.roll(t, half, 1) * sin_hi


def _in_proj_kernel(*refs, mix_width, diff):
    if diff:
        (x_ref, g_ref, w_ref, mg_ref, qg_ref, kg_ref, cs_ref, spread_ref, offset_ref,
         q_out, k_out, v_out, m_out) = refs
        hi = cs_ref[...].astype(BF16)
        mid, lo = _split_bf16(cs_ref[...] - hi.astype(F32))
        spread = spread_ref[...].astype(BF16)
        tables = _dot(hi, spread) + _dot(mid, spread) + _dot(lo, spread) + offset_ref[...]
        rope = functools.partial(_rope, cos=tables[:, :LANES], sin_lo=tables[:, LANES:2 * LANES],
                                 sin_hi=tables[:, 2 * LANES:])
    else:
        x_ref, g_ref, w_ref, mg_ref, q_out, k_out, v_out, m_out = refs
    h = _rms(x_ref[...], g_ref[...]).astype(BF16)
    ones = _group_ones(LANES, HEAD_DIM)
    chunk = 2 * LANES
    for c in range(mix_width // chunk):
        lo = c * chunk
        q = _dot(h, w_ref[:, lo:lo + chunk])
        k = _dot(h, w_ref[:, mix_width + lo:mix_width + lo + chunk])
        for half in range(2):
            sl = slice(half * LANES, (half + 1) * LANES)
            dst = slice(lo + half * LANES, lo + (half + 1) * LANES)
            qh, kh = q[:, sl], k[:, sl]
            if diff:
                qh = rope(_head_rms(qh, qg_ref[...], ones))
                kh = rope(_head_rms(kh, kg_ref[...], ones))
            q_out[:, dst] = (qh * (SCALE * LOG2E)).astype(BF16)
            k_out[:, dst] = kh.astype(BF16)
    v_out[...] = _dot(h, w_ref[:, 2 * mix_width:3 * mix_width]).astype(BF16)
    qm = _dot(h, w_ref[:, 3 * mix_width:])
    mem_width = qm.shape[-1]
    for c in range(mem_width // LANES):
        sl = slice(c * LANES, (c + 1) * LANES)
        m_out[:, sl] = (_head_rms(qm[:, sl], mg_ref[...], ones) * SCALE).astype(BF16)


def _in_proj(x2d, g, w, layer, mem_q_gain, diff_args, *, mix_width):
    rows, d_model = x2d.shape
    in_width = w.shape[2]
    mem_width = in_width - 3 * mix_width
    tm = ROW_TILE
    row_spec = lambda n: pl.BlockSpec((tm, n), lambda i: (i, 0))
    const_spec = lambda a: pl.BlockSpec(a.shape, lambda i: (0, 0))
    diff = diff_args is not None
    in_specs = [row_spec(d_model), const_spec(g), _layer_spec(w, layer), const_spec(mem_q_gain)]
    args = [x2d, g, w, mem_q_gain]
    if diff:
        q_gain, k_gain, cos_sin, spread, offset = diff_args
        in_specs += [const_spec(q_gain), const_spec(k_gain), row_spec(cos_sin.shape[1]),
                     const_spec(spread), const_spec(offset)]
        args += [q_gain, k_gain, cos_sin, spread, offset]
    mix = jax.ShapeDtypeStruct((rows, mix_width), BF16)
    return pl.pallas_call(
        functools.partial(_in_proj_kernel, mix_width=mix_width, diff=diff),
        out_shape=(mix, mix, mix, jax.ShapeDtypeStruct((rows, mem_width), BF16)),
        grid=(rows // tm,),
        in_specs=in_specs,
        out_specs=(row_spec(mix_width), row_spec(mix_width), row_spec(mix_width),
                   row_spec(mem_width)),
        compiler_params=_params("parallel"),
        name="in_proj_diff" if diff else "in_proj_sb",
    )(*args)


def _mem_attn(q, k, v):
    lane = lax.broadcasted_iota(jnp.int32, (1, q.shape[-1]), 1) // HEAD_DIM
    out = jnp.zeros(q.shape, F32)
    for hd in range(N_MEM_HEADS):
        sel = lane == hd
        s = _dot_nt(jnp.where(sel, q, 0), k)
        p = jnp.exp(s - jnp.max(s, axis=-1, keepdims=True))
        p = p / jnp.sum(p, axis=-1, keepdims=True)
        out = out + _dot(p.astype(BF16), jnp.where(sel, v, 0))
    return out.astype(BF16)


def _sb_kernel(q_ref, k_ref, v_ref, o_ref):
    t = SB_TILE
    lane = lax.broadcasted_iota(jnp.int32, (1, LANES), 1)
    first = lane < HEAD_DIM
    key = lax.broadcasted_iota(jnp.int32, (t, t), 0)
    qry = lax.broadcasted_iota(jnp.int32, (t, t), 1)
    later = (qry > key).astype(BF16)
    strict = key < qry

    def block(j, q_heads, carry, mask, live):
        spent_a, spent_b, o = carry
        kb = k_ref[0, pl.ds(j * t, t), :]
        vb = v_ref[0, pl.ds(j * t, t), :]
        if live is not None:
            vb = jnp.where(live, vb, 0)
        v_heads = (jnp.where(first, vb, 0), jnp.where(first, 0, vb))
        spent = [spent_a, spent_b]
        for hd in range(2):
            z = _dot_nt(kb, q_heads[hd])
            if mask is not None:
                z = jnp.where(mask, z, NEG_BIG)
            sp = jnp.maximum(z, jnp.log2(1.0 + jnp.exp2(jnp.minimum(z, SP_CLAMP))))
            after = _dot(later, sp.astype(BF16)) + spent[hd]
            w = jnp.exp2(z - sp - after)
            o = o + lax.dot_general(v_heads[hd], w.astype(BF16), (((0,), (0,)), ((), ())),
                                    preferred_element_type=F32)
            spent[hd] = spent[hd] + jnp.sum(sp, axis=0, keepdims=True)
        return spent[0], spent[1], o

    def stick_left(spent_a, spent_b):
        return (jnp.min(jnp.minimum(spent_a, spent_b)) < SB_DONE).astype(jnp.int32)

    zero = jnp.zeros((1, t), F32)
    started = []
    for sub in range(q_ref.shape[1] // t):
        i = pl.program_id(2) * (q_ref.shape[1] // t) + sub
        q = q_ref[0, sub * t:(sub + 1) * t, :]
        q_heads = (jnp.where(first, q, 0), jnp.where(first, 0, q))
        carry = block(i, q_heads, (zero, zero, jnp.zeros((LANES, t), F32)), strict, None)
        carry = block(jnp.maximum(i - 1, 0), q_heads, carry, None, i > 0)
        started.append((i, q_heads, carry))

    for sub, (i, q_heads, carry) in enumerate(started):
        def earlier(state, i=i, q_heads=q_heads):
            n, _, carry = state
            carry = block(i - 1 - n, q_heads, carry, None, None)
            return n + 1, stick_left(carry[0], carry[1]), carry

        state = (jnp.int32(1), stick_left(carry[0], carry[1]), carry)
        state = lax.while_loop(lambda st, i=i: (st[0] < i) & (st[1] > 0), earlier, state)
        o_ref[0, sub * t:(sub + 1) * t, :] = state[2][2].T.astype(BF16)


def _token_attn(kernel, name, t, q, k, v, extra=(), scratch=()):
    b, s, width = q.shape
    q_spec = pl.BlockSpec((1, t, LANES), lambda bi, p, i: (bi, i, p))
    kv_spec = pl.BlockSpec((1, s, LANES), lambda bi, p, i: (bi, 0, p))
    extra_specs = [pl.BlockSpec(a.shape, lambda bi, p, i: (0, 0)) for a in extra]
    return pl.pallas_call(
        kernel,
        out_shape=jax.ShapeDtypeStruct((b, s, width), BF16),
        grid=(b, width // LANES, s // t),
        in_specs=[q_spec, kv_spec, kv_spec] + extra_specs,
        out_specs=q_spec,
        scratch_shapes=list(scratch),
        compiler_params=_params("parallel", "parallel", "arbitrary"),
        name=name,
    )(q, k, v, *extra)


def _diff_kernel(q_ref, k_ref, v_ref, lam_ref, g_ref, o_ref, *s_refs, out_scale):
    t = q_ref.shape[1]
    i = pl.program_id(2)
    q = q_ref[0]
    lane = lax.broadcasted_iota(jnp.int32, (1, LANES), 1)
    first = lane < HEAD_DIM
    q_maps = (jnp.where(first, q, 0), jnp.where(first, 0, q))
    key = lax.broadcasted_iota(jnp.int32, (t, t), 0)
    qry = lax.broadcasted_iota(jnp.int32, (t, t), 1)
    causal = key <= qry

    def scores(j, slot, mask):
        kb = k_ref[0, pl.ds(j * t, t), :]
        tops = []
        for c in range(2):
            s = _dot_nt(kb, q_maps[c])
            if mask is not None:
                s = jnp.where(mask, s, NEG_BIG)
            s_refs[slot][c] = s
            tops.append(jnp.max(s, axis=0, keepdims=True))
        return tuple(tops)

    def absorb(j, slot, tops, carry):
        vb = v_ref[0, pl.ds(j * t, t), :]
        new = []
        for c in range(2):
            m, l, acc = carry[c]
            m_new = jnp.maximum(m, tops[c])
            alpha = jnp.exp2(m - m_new)
            p = jnp.exp2(s_refs[slot][c] - m_new)
            l = alpha * l + jnp.sum(p, axis=0, keepdims=True)
            pv = lax.dot_general(vb, p.astype(BF16), (((0,), (0,)), ((), ())),
                                 preferred_element_type=F32)
            new.append((m_new, l, alpha * acc + pv))
        return tuple(new)

    def step(n, slot, state):
        tops, carry = state
        tops_next = scores(n, 1 - slot, None)
        return tops_next, absorb(jnp.where(n == 0, i, n - 1), slot, tops, carry)

    def pair(n2, state):
        return step(2 * n2 + 1, 1, step(2 * n2, 0, state))

    def tail_even(state):
        tops, carry = state
        return absorb(jnp.maximum(i - 1, 0), 0, tops, carry)

    def tail_odd(state):
        tops, carry = step(i - 1, 0, state)
        return absorb(i - 1, 1, tops, carry)

    init = (jnp.full((1, t), NEG_BIG, F32), jnp.zeros((1, t), F32), jnp.zeros((LANES, t), F32))
    state = lax.fori_loop(0, i // 2, pair, (scores(i, 0, causal), (init, init)))
    carry = lax.cond(i % 2 == 1, tail_odd, tail_even, state)
    (_, l1, acc1), (_, l2, acc2) = carry
    o = acc1 / l1 - lam_ref[...] * (acc2 / l2)
    o = o * lax.rsqrt(jnp.mean(o * o, axis=0, keepdims=True) + EPS)
    o_ref[0] = (o.T * (g_ref[...] * out_scale)).astype(BF16)


def _top2_route(logits):
    lane = lax.broadcasted_iota(jnp.int32, logits.shape, 1)
    lg = jnp.where(lane < N_EXPERTS, logits, -jnp.inf)
    m1 = jnp.max(lg, axis=-1, keepdims=True)
    i1 = jnp.min(jnp.where(lg == m1, lane, LANES), axis=-1, keepdims=True)
    lg2 = jnp.where(lane == i1, -jnp.inf, lg)
    m2 = jnp.max(lg2, axis=-1, keepdims=True)
    i2 = jnp.min(jnp.where(lg2 == m2, lane, LANES), axis=-1, keepdims=True)
    e = jnp.exp(m2 - m1)
    w1 = 1.0 / (1.0 + e)
    fields = (i1.astype(F32), i2.astype(F32), w1, e * w1)
    out = jnp.zeros(logits.shape, F32)
    for n, val in enumerate(fields):
        out = jnp.where(lane == n, val, out)
    return out


def _out_proj_kernel(*refs, moe):
    if moe:
        mix_ref, qm_ref, km_ref, vm_ref, w_ref, x_ref, g_ref, wr_ref, x_out, h_out, c_out = refs
    else:
        mix_ref, qm_ref, km_ref, vm_ref, w_ref, x_ref, g_ref, x_out, h_out = refs
    mix_width = mix_ref.shape[-1]
    o_mem = _mem_attn(qm_ref[...], km_ref[...], vm_ref[...])
    x = x_ref[...] + _dot(mix_ref[...], w_ref[:mix_width, :]) + _dot(o_mem, w_ref[mix_width:, :])
    x_out[...] = x
    h = _rms(x, g_ref[...])
    h_out[...] = h.astype(h_out.dtype)
    if moe:
        h_hi, h_lo = _split_bf16(h)
        w_hi, w_lo = _split_bf16(wr_ref[...])
        c_out[...] = _top2_route(_dot(h_hi, w_hi) + _dot(h_hi, w_lo) + _dot(h_lo, w_hi))


def _out_proj(o_mix, qm, km, vm, w, layer, x2d, g, w_router):
    rows, d_model = x2d.shape
    tm = ROW_TILE
    seq = rows // km.shape[1]
    row_spec = lambda n: pl.BlockSpec((tm, n), lambda i: (i, 0))
    const_spec = lambda a: pl.BlockSpec(a.shape, lambda i: (0, 0))
    mem_spec = pl.BlockSpec((None, None) + km.shape[2:], lambda i: (layer, i * tm // seq, 0, 0))
    moe = w_router is not None
    in_specs = [row_spec(o_mix.shape[1]), row_spec(qm.shape[1]), mem_spec, mem_spec,
                _layer_spec(w, layer), row_spec(d_model), const_spec(g)]
    args = [o_mix, qm, km, vm, w, x2d, g]
    out_shape = [jax.ShapeDtypeStruct((rows, d_model), F32),
                 jax.ShapeDtypeStruct((rows, d_model), F32 if moe else BF16)]
    out_specs = [row_spec(d_model), row_spec(d_model)]
    if moe:
        in_specs.append(const_spec(w_router))
        args.append(w_router)
        out_shape.append(jax.ShapeDtypeStruct((rows, LANES), F32))
        out_specs.append(row_spec(LANES))
    return pl.pallas_call(
        functools.partial(_out_proj_kernel, moe=moe),
        out_shape=tuple(out_shape),
        grid=(rows // tm,),
        in_specs=in_specs,
        out_specs=tuple(out_specs),
        compiler_params=_params("parallel"),
        name="out_proj_moe" if moe else "out_proj",
    )(*args)


def _swiglu_chunk(h, wg, wu, wd):
    g = _dot(h, wg)
    u = _dot(h, wu)
    return _dot((g * jax.nn.sigmoid(g) * u).astype(BF16), wd)


def _dense_ffn_kernel(h_ref, wgu_ref, wd_ref, x_ref, o_ref):
    d_ff = wd_ref.shape[0]
    o_ref[...] = x_ref[...] + _swiglu_chunk(h_ref[...], wgu_ref[:, :d_ff], wgu_ref[:, d_ff:],
                                            wd_ref[...])


def _dense_ffn(h, w_gate_up, w_down, layer, x2d):
    rows, d_model = x2d.shape
    tm = FFN_ROW_TILE
    resident = lambda a: _layer_spec(a, layer, pipeline_mode=pl.Buffered(1))
    row_spec = pl.BlockSpec((tm, d_model), lambda i: (i, 0))
    return pl.pallas_call(
        _dense_ffn_kernel,
        out_shape=jax.ShapeDtypeStruct((rows, d_model), F32),
        grid=(rows // tm,),
        in_specs=[row_spec, resident(w_gate_up), resident(w_down), row_spec],
        out_specs=row_spec,
        compiler_params=_params("parallel"),
        name="dense_ffn",
    )(h, w_gate_up, w_down, x2d)


def _route_plan(route, tm):
    n_tokens = route.shape[0]
    expert = jnp.concatenate([route[:, 0], route[:, 1]]).astype(jnp.int32)
    onehot = (expert[:, None] == jnp.arange(N_EXPERTS, dtype=jnp.int32)[None, :]).astype(jnp.int32)
    csum = jnp.cumsum(onehot, axis=0)
    tiles = (csum[-1] + tm - 1) // tm
    tile_end = jnp.cumsum(tiles)
    start = (tile_end - tiles) * tm
    pos = jnp.sum(onehot * (start[None, :] + csum - 1), axis=1)
    n_tiles = 2 * n_tokens // tm + N_EXPERTS
    tile_expert = jnp.sum(jnp.arange(n_tiles, dtype=jnp.int32)[:, None] >= tile_end[None, :], axis=1)
    n_used = tile_end[-1]
    tile_expert = jnp.minimum(tile_expert, tile_expert[n_used - 1])
    pos = pos.astype(jnp.int32)
    token = jnp.arange(2 * n_tokens, dtype=jnp.int32) % n_tokens
    src = jnp.zeros((n_tiles * tm,), jnp.int32).at[pos].set(token, unique_indices=True)
    return pos, src, jnp.concatenate([tile_expert, n_used[None]]).astype(jnp.int32)


def _for_each_row(n_rows, fn):
    def trip(g, c):
        for u in range(DMA_UNROLL):
            fn(g, u)
        return c

    lax.fori_loop(0, n_rows // DMA_UNROLL, trip, 0)


def _expert_ffn_kernel(plan_ref, src_ref, nxt_ref, h_ref, wg_ref, wu_ref, wd_ref, ys_ref,
                       hs_ref, hb_ref, acc_ref, sem, *, nf):
    i, f = pl.program_id(0), pl.program_id(1)
    n_tiles = pl.num_programs(0)
    n_used = plan_ref[n_tiles]
    used = i < n_used
    fetched = (i == 0) | (i - 1 < n_used)
    tm, d_model = hb_ref.shape
    share = tm // nf // DMA_UNROLL
    slot = i % 2

    def row_copy(idx_ref, g, u, s):
        token = idx_ref[0, 0, g * DMA_UNROLL + u]
        return pltpu.make_async_copy(h_ref.at[pl.ds(token, 1)],
                                     hs_ref.at[s, g, pl.ds(u, 1)], sem.at[s])

    @pl.when((i == 0) & (f == 0))
    def _():
        _for_each_row(tm, lambda g, u: row_copy(src_ref, g, u, 0).start())

    @pl.when(fetched & (f == 0))
    def _():
        _for_each_row(tm, lambda g, u: row_copy(src_ref, g, u, slot).wait())

    @pl.when(used & (f == 0))
    def _():
        hb_ref[...] = hs_ref[slot].reshape(tm, d_model).astype(BF16)
        acc_ref[...] = jnp.zeros_like(acc_ref)

    @pl.when(used)
    def _():
        for g in range(share):
            for u in range(DMA_UNROLL):
                row_copy(nxt_ref, f * share + g, u, 1 - slot).start()
        acc_ref[...] += _swiglu_chunk(hb_ref[...], wg_ref[0], wu_ref[0], wd_ref[0])

    @pl.when(used & (f == nf - 1))
    def _():
        ys_ref[...] = acc_ref[...]

    @pl.when(used & (i == n_tiles - 1) & (f == nf - 1))
    def _():
        _for_each_row(tm, lambda g, u: row_copy(nxt_ref, g, u, 1 - slot).wait())

    @pl.when(jnp.logical_not(used) & (f == nf - 1))
    def _():
        ys_ref[...] = jnp.zeros_like(ys_ref)


def _expert_ffn(h, src, plan, w_gate_up, w_down, layer, *, ff_tile):
    d_model = h.shape[1]
    d_ff = w_down.shape[2]
    nf = d_ff // ff_tile
    tm = MOE_ROW_TILE
    n_tiles = plan.shape[0] - 1
    n_rows = n_tiles * tm

    def chunk_of(i, f, plan_ref):
        return jnp.where(i < plan_ref[n_tiles], f, nf - 1)

    grid_spec = pltpu.PrefetchScalarGridSpec(
        num_scalar_prefetch=1,
        grid=(n_tiles, nf),
        in_specs=[
            pl.BlockSpec((1, 1, tm), lambda i, f, p: (i, 0, 0), memory_space=pltpu.SMEM),
            pl.BlockSpec((1, 1, tm), lambda i, f, p: (jnp.minimum(i + 1, n_tiles - 1), 0, 0),
                         memory_space=pltpu.SMEM),
            pl.BlockSpec(memory_space=pl.ANY),
            pl.BlockSpec((None, 1, d_model, ff_tile),
                         lambda i, f, p: (layer, p[i], 0, chunk_of(i, f, p))),
            pl.BlockSpec((None, 1, d_model, ff_tile),
                         lambda i, f, p: (layer, p[i], 0, chunk_of(i, f, p) + nf)),
            pl.BlockSpec((None, 1, ff_tile, d_model),
                         lambda i, f, p: (layer, p[i], chunk_of(i, f, p), 0)),
        ],
        out_specs=pl.BlockSpec((tm, d_model), lambda i, f, p: (i, 0)),
        scratch_shapes=[pltpu.VMEM((2, tm // DMA_UNROLL, DMA_UNROLL, d_model), F32),
                        pltpu.VMEM((tm, d_model), BF16),
                        pltpu.VMEM((tm, d_model), F32), pltpu.SemaphoreType.DMA((2,))],
    )
    src3 = src.reshape(n_tiles, 1, tm)
    return pl.pallas_call(
        functools.partial(_expert_ffn_kernel, nf=nf),
        out_shape=jax.ShapeDtypeStruct((n_rows, d_model), F32),
        grid_spec=grid_spec,
        compiler_params=_params("arbitrary", "arbitrary"),
        name="moe_experts",
    )(plan, src3, src3, h, w_gate_up, w_gate_up, w_down)


def _combine_kernel(p1_ref, p2_ref, ys_ref, route_ref, x_ref, o_ref, buf_ref, sem):
    chunk, d_model = x_ref.shape

    def row_copy(k, pos_ref, g, u):
        row = pos_ref[0, 0, g * DMA_UNROLL + u]
        return pltpu.make_async_copy(ys_ref.at[pl.ds(row, 1)],
                                     buf_ref.at[k, g, pl.ds(u, 1)], sem.at[k])

    def start(g, u):
        row_copy(0, p1_ref, g, u).start()
        row_copy(1, p2_ref, g, u).start()

    def wait(g, u):
        row_copy(0, p1_ref, g, u).wait()
        row_copy(1, p2_ref, g, u).wait()

    _for_each_row(chunk, start)
    _for_each_row(chunk, wait)
    route = route_ref[...]
    y1 = buf_ref[0].reshape(chunk, d_model)
    y2 = buf_ref[1].reshape(chunk, d_model)
    o_ref[...] = x_ref[...] + route[:, 2:3] * y1 + route[:, 3:4] * y2


def _combine(ys, pos, route, x2d):
    rows, d_model = x2d.shape
    chunk = ROUTE_CHUNK
    n_chunks = rows // chunk
    pos3 = pos.reshape(2 * n_chunks, 1, chunk)
    smem_spec = lambda off: pl.BlockSpec((1, 1, chunk), lambda c: (c + off, 0, 0), memory_space=pltpu.SMEM)
    row_spec = lambda n: pl.BlockSpec((chunk, n), lambda c: (c, 0))
    return pl.pallas_call(
        _combine_kernel,
        out_shape=jax.ShapeDtypeStruct((rows, d_model), F32),
        grid=(n_chunks,),
        in_specs=[smem_spec(0), smem_spec(n_chunks), pl.BlockSpec(memory_space=pl.ANY),
                  row_spec(LANES), row_spec(d_model)],
        out_specs=row_spec(d_model),
        scratch_shapes=[pltpu.VMEM((2, chunk // DMA_UNROLL, DMA_UNROLL, d_model), F32),
                        pltpu.SemaphoreType.DMA((2,))],
        compiler_params=_params("arbitrary"),
        name="moe_combine",
    )(pos3, pos3, ys, route, x2d)


def _moe_ffn(h, route, w_gate_up, w_down, layer, x2d, *, ff_tile):
    pos, src, plan = _route_plan(route, MOE_ROW_TILE)
    ys = _expert_ffn(h, src, plan, w_gate_up, w_down, layer, ff_tile=ff_tile)
    return _combine(ys, pos, route, x2d)


def _rope_tables(positions):
    half = ROPE_DIM // 2
    inv_freq = ROPE_THETA ** (-jnp.arange(0, ROPE_DIM, 2, dtype=F32) / ROPE_DIM)
    ang = positions.astype(F32).reshape(-1, 1) * inv_freq
    cos_sin = jnp.concatenate([jnp.cos(ang), jnp.sin(ang)], axis=-1)
    spread = np.zeros((ROPE_DIM, 3 * LANES), np.float32)
    offset = np.zeros((1, 3 * LANES), np.float32)
    for lane in range(LANES):
        m = lane % HEAD_DIM
        if m < ROPE_DIM:
            spread[m % half, lane] = 1.0
        else:
            offset[0, lane] = 1.0
        if m < half:
            spread[half + m, LANES + lane] = -1.0
        elif m < ROPE_DIM:
            spread[half + m - half, 2 * LANES + lane] = 1.0
    return cos_sin, jnp.asarray(spread), jnp.asarray(offset)


def _tile_gain(g, width):
    return jnp.tile(g.astype(F32), width // g.shape[-1]).reshape(1, width)


def _ff_tile(d_ff, limit):
    for step in (MXU_WIDTH, LANES):
        fits = [t for t in range(step, limit + 1, step) if d_ff % t == 0 and t > step]
        if fits:
            return max(fits)
    return LANES


def kernel(x, mem, positions, attn_norm, w_in, w_out, mem_norm, w_mem_kv, mem_q_norm, mem_k_norm,
           diff_q_norm, diff_k_norm, diff_lambda, diff_subln, ffn_norm, dense_w_gate_up,
           dense_w_down, w_router, moe_w_gate_up, moe_w_down):
    b, s, d_model = x.shape
    depth = w_in.shape[0]
    mem_len = mem.shape[1]
    mem_width = w_mem_kv.shape[-1] // 2
    mix_width = (w_in.shape[-1] - mem_width) // 3
    rows = b * s
    assert (s % max(ROW_TILE, FFN_ROW_TILE, SB_GROUP * SB_TILE, DIFF_TILE) == 0
            and mix_width % (2 * LANES) == 0)

    row = lambda a: a.astype(F32).reshape(1, -1)
    k_gain = jnp.stack([_tile_gain(mem_k_norm[i], mem_width) for i in range(depth)])
    km, vm = _mem_kv(mem.reshape(b * mem_len, d_model), row(mem_norm), w_mem_kv.astype(BF16), k_gain)
    km = km.reshape(depth, b, mem_len, mem_width)
    vm = vm.reshape(depth, b, mem_len, mem_width)
    rope_tables = _rope_tables(positions)
    w_in, w_out, dense_w_gate_up, dense_w_down, moe_w_gate_up, moe_w_down = (
        w.astype(BF16) for w in (w_in, w_out, dense_w_gate_up, dense_w_down, moe_w_gate_up, moe_w_down))

    x2d = x.reshape(rows, d_model)
    for i in range(depth):
        j = i // 2
        is_diff = i % 2 == 1
        diff_args = None
        if is_diff:
            diff_args = (_tile_gain(diff_q_norm[j], LANES), _tile_gain(diff_k_norm[j], LANES),
                         *rope_tables)
        q, k, v, qm = _in_proj(x2d, row(attn_norm[i]), w_in, i,
                               _tile_gain(mem_q_norm[i], LANES), diff_args, mix_width=mix_width)
        to3 = lambda a: a.reshape(b, s, a.shape[-1])
        if is_diff:
            lam_init = 0.8 - 0.6 * math.exp(-0.3 * i)
            lp = diff_lambda[j].astype(F32)
            lam = jnp.exp(jnp.sum(lp[0] * lp[1])) - jnp.exp(jnp.sum(lp[2] * lp[3])) + lam_init
            o_mix = _token_attn(functools.partial(_diff_kernel, out_scale=1.0 - lam_init), "diff_attn",
                                DIFF_TILE, to3(q), to3(k), to3(v),
                                extra=(lam.reshape(1, 1), row(diff_subln[j])),
                                scratch=(pltpu.VMEM((2, DIFF_TILE, DIFF_TILE), F32),) * 2)
        else:
            o_mix = _token_attn(_sb_kernel, "sb_attn", SB_GROUP * SB_TILE, to3(q), to3(k), to3(v))
        router = None
        if is_diff:
            router = jnp.pad(w_router[j].astype(F32), ((0, 0), (0, LANES - w_router.shape[-1])))
        outs = _out_proj(o_mix.reshape(rows, mix_width), qm, km, vm, w_out, i, x2d,
                         row(ffn_norm[i]), router)
        if is_diff:
            x2d, h, route = outs
            x2d = _moe_ffn(h, route, moe_w_gate_up, moe_w_down, j, x2d,
                           ff_tile=_ff_tile(moe_w_down.shape[-2], 2048))
        else:
            x2d, h = outs
            x2d = _dense_ffn(h, dense_w_gate_up, dense_w_down, j, x2d)
    return x2d.reshape(b, s, d_model)
```

```python
import functools
import math

import jax
import jax.numpy as jnp
import numpy as np
from jax import lax
from jax.experimental import pallas as pl
from jax.experimental.pallas import tpu as pltpu

F32 = jnp.float32
BF16 = jnp.bfloat16

HEAD_DIM = 64
LANES = 128
MXU_WIDTH = 256
N_MEM_HEADS = 4
ROPE_DIM = HEAD_DIM // 4
ROPE_THETA = 500000.0
N_EXPERTS = 8
EPS = 1e-6
NEG_BIG = -1e30
SCALE = HEAD_DIM ** -0.5
LOG2E = math.log2(math.e)
SP_CLAMP = 126.0
SB_DONE = 104.0 * LOG2E
VMEM_LIMIT = 48 * 1024 * 1024

ROW_TILE = 1024
FFN_ROW_TILE = 512
SB_TILE = 256
SB_GROUP = 4
DIFF_TILE = 1024
MOE_ROW_TILE = 512
ROUTE_CHUNK = 512
DMA_UNROLL = 8


def _params(*sem):
    return pltpu.CompilerParams(dimension_semantics=sem, vmem_limit_bytes=VMEM_LIMIT)


def _layer_spec(stacked, layer, **kwargs):
    return pl.BlockSpec((None,) + stacked.shape[1:], lambda i: (layer, 0, 0), **kwargs)


def _split_bf16(x):
    hi = x.astype(BF16)
    lo = (x - hi.astype(F32)).astype(BF16)
    return hi, lo


def _dot(a, b):
    return jnp.dot(a, b, preferred_element_type=F32)


def _dot_nt(a, b):
    return lax.dot_general(a, b, (((1,), (1,)), ((), ())), preferred_element_type=F32)


def _dot_split(x, m):
    hi, lo = _split_bf16(x)
    return _dot(hi, m) + _dot(lo, m)


def _group_ones(n, group):
    r = lax.broadcasted_iota(jnp.int32, (n, n), 0) // group
    c = lax.broadcasted_iota(jnp.int32, (n, n), 1) // group
    return (r == c).astype(BF16)


def _head_rms(t, gain, ones):
    ss = _dot_split(t * t, ones)
    return t * lax.rsqrt(ss * (1.0 / HEAD_DIM) + EPS) * gain


def _rms(x, g):
    return x * lax.rsqrt(jnp.mean(x * x, axis=-1, keepdims=True) + EPS) * g


def _mem_kv_kernel(mem_ref, g_ref, w_ref, kg_ref, k_out, v_out):
    width = k_out.shape[-1]
    mem_n = _rms(mem_ref[...], g_ref[...]).astype(BF16)
    kv = _dot(mem_n, w_ref[0])
    ones = _group_ones(width, HEAD_DIM)
    k_out[0] = _head_rms(kv[:, :width], kg_ref[0], ones).astype(BF16)
    v_out[0] = kv[:, width:].astype(BF16)


def _mem_kv(mem2d, mem_norm, w_mem_kv, k_gain):
    depth, d_model, two_w = w_mem_kv.shape
    width = two_w // 2
    rows = mem2d.shape[0]
    out = jax.ShapeDtypeStruct((depth, rows, width), BF16)
    return pl.pallas_call(
        _mem_kv_kernel,
        out_shape=(out, out),
        grid=(depth,),
        in_specs=[
            pl.BlockSpec((rows, d_model), lambda i: (0, 0)),
            pl.BlockSpec((1, d_model), lambda i: (0, 0)),
            pl.BlockSpec((1, d_model, two_w), lambda i: (i, 0, 0)),
            pl.BlockSpec((1, 1, width), lambda i: (i, 0, 0)),
        ],
        out_specs=(pl.BlockSpec((1, rows, width), lambda i: (i, 0, 0)),
                   pl.BlockSpec((1, rows, width), lambda i: (i, 0, 0))),
        compiler_params=_params("arbitrary"),
        name="mem_kv",
    )(mem2d, mem_norm, w_mem_kv, k_gain)


def _rope(t, cos, sin_lo, sin_hi):
    n = t.shape[-1]
    half = ROPE_DIM // 2
    return t * cos + pltpu.roll(t, n - half, 1) * sin_lo + pltpu.roll(t, half, 1) * sin_hi


def _in_proj_kernel(*refs, mix_width, diff):
    if diff:
        (x_ref, g_ref, w_ref, mg_ref, qg_ref, kg_ref, cs_ref, spread_ref, offset_ref,
         q_out, k_out, v_out, m_out) = refs
        hi = cs_ref[...].astype(BF16)
        mid, lo = _split_bf16(cs_ref[...] - hi.astype(F32))
        spread = spread_ref[...].astype(BF16)
        tables = _dot(hi, spread) + _dot(mid, spread) + _dot(lo, spread) + offset_ref[...]
        rope = functools.partial(_rope, cos=tables[:, :LANES], sin_lo=tables[:, LANES:2 * LANES],
                                 sin_hi=tables[:, 2 * LANES:])
    else:
        x_ref, g_ref, w_ref, mg_ref, q_out, k_out, v_out, m_out = refs
    h = _rms(x_ref[...], g_ref[...]).astype(BF16)
    ones = _group_ones(LANES, HEAD_DIM)
    chunk = 2 * LANES
    for c in range(mix_width // chunk):
        lo = c * chunk
        q = _dot(h, w_ref[:, lo:lo + chunk])
        k = _dot(h, w_ref[:, mix_width + lo:mix_width + lo + chunk])
        for half in range(2):
            sl = slice(half * LANES, (half + 1) * LANES)
            dst = slice(lo + half * LANES, lo + (half + 1) * LANES)
            qh, kh = q[:, sl], k[:, sl]
            if diff:
                qh = rope(_head_rms(qh, qg_ref[...], ones))
                kh = rope(_head_rms(kh, kg_ref[...], ones))
            q_out[:, dst] = (qh * (SCALE * LOG2E)).astype(BF16)
            k_out[:, dst] = kh.astype(BF16)
    v_out[...] = _dot(h, w_ref[:, 2 * mix_width:3 * mix_width]).astype(BF16)
    qm = _dot(h, w_ref[:, 3 * mix_width:])
    mem_width = qm.shape[-1]
    for c in range(mem_width // LANES):
        sl = slice(c * LANES, (c + 1) * LANES)
        m_out[:, sl] = (_head_rms(qm[:, sl], mg_ref[...], ones) * SCALE).astype(BF16)


def _in_proj(x2d, g, w, layer, mem_q_gain, diff_args, *, mix_width):
    rows, d_model = x2d.shape
    in_width = w.shape[2]
    mem_width = in_width - 3 * mix_width
    tm = ROW_TILE
    row_spec = lambda n: pl.BlockSpec((tm, n), lambda i: (i, 0))
    const_spec = lambda a: pl.BlockSpec(a.shape, lambda i: (0, 0))
    diff = diff_args is not None
    in_specs = [row_spec(d_model), const_spec(g), _layer_spec(w, layer), const_spec(mem_q_gain)]
    args = [x2d, g, w, mem_q_gain]
    if diff:
        q_gain, k_gain, cos_sin, spread, offset = diff_args
        in_specs += [const_spec(q_gain), const_spec(k_gain), row_spec(cos_sin.shape[1]),
                     const_spec(spread), const_spec(offset)]
        args += [q_gain, k_gain, cos_sin, spread, offset]
    mix = jax.ShapeDtypeStruct((rows, mix_width), BF16)
    return pl.pallas_call(
        functools.partial(_in_proj_kernel, mix_width=mix_width, diff=diff),
        out_shape=(mix, mix, mix, jax.ShapeDtypeStruct((rows, mem_width), BF16)),
        grid=(rows // tm,),
        in_specs=in_specs,
        out_specs=(row_spec(mix_width), row_spec(mix_width), row_spec(mix_width),
                   row_spec(mem_width)),
        compiler_params=_params("parallel"),
        name="in_proj_diff" if diff else "in_proj_sb",
    )(*args)


def _mem_attn(q, k, v):
    lane = lax.broadcasted_iota(jnp.int32, (1, q.shape[-1]), 1) // HEAD_DIM
    out = jnp.zeros(q.shape, F32)
    for hd in range(N_MEM_HEADS):
        sel = lane == hd
        s = _dot_nt(jnp.where(sel, q, 0), k)
        p = jnp.exp(s - jnp.max(s, axis=-1, keepdims=True))
        p = p / jnp.sum(p, axis=-1, keepdims=True)
        out = out + _dot(p.astype(BF16), jnp.where(sel, v, 0))
    return out.astype(BF16)


def _sb_kernel(q_ref, k_ref, v_ref, o_ref):
    t = SB_TILE
    lane = lax.broadcasted_iota(jnp.int32, (1, LANES), 1)
    first = lane < HEAD_DIM
    key = lax.broadcasted_iota(jnp.int32, (t, t), 0)
    qry = lax.broadcasted_iota(jnp.int32, (t, t), 1)
    later = (qry > key).astype(BF16)
    strict = key < qry

    def block(j, q_heads, carry, mask, live):
        spent_a, spent_b, o = carry
        kb = k_ref[0, pl.ds(j * t, t), :]
        vb = v_ref[0, pl.ds(j * t, t), :]
        if live is not None:
            vb = jnp.where(live, vb, 0)
        v_heads = (jnp.where(first, vb, 0), jnp.where(first, 0, vb))
        spent = [spent_a, spent_b]
        for hd in range(2):
            z = _dot_nt(kb, q_heads[hd])
            if mask is not None:
                z = jnp.where(mask, z, NEG_BIG)
            sp = jnp.maximum(z, jnp.log2(1.0 + jnp.exp2(jnp.minimum(z, SP_CLAMP))))
            after = _dot(later, sp.astype(BF16)) + spent[hd]
            w = jnp.exp2(z - sp - after)
            o = o + lax.dot_general(v_heads[hd], w.astype(BF16), (((0,), (0,)), ((), ())),
                                    preferred_element_type=F32)
            spent[hd] = spent[hd] + jnp.sum(sp, axis=0, keepdims=True)
        return spent[0], spent[1], o

    def stick_left(spent_a, spent_b):
        return (jnp.min(jnp.minimum(spent_a, spent_b)) < SB_DONE).astype(jnp.int32)

    zero = jnp.zeros((1, t), F32)
    started = []
    for sub in range(q_ref.shape[1] // t):
        i = pl.program_id(2) * (q_ref.shape[1] // t) + sub
        q = q_ref[0, sub * t:(sub + 1) * t, :]
        q_heads = (jnp.where(first, q, 0), jnp.where(first, 0, q))
        carry = block(i, q_heads, (zero, zero, jnp.zeros((LANES, t), F32)), strict, None)
        carry = block(jnp.maximum(i - 1, 0), q_heads, carry, None, i > 0)
        started.append((i, q_heads, carry))

    for sub, (i, q_heads, carry) in enumerate(started):
        def earlier(state, i=i, q_heads=q_heads):
            n, _, carry = state
            carry = block(i - 1 - n, q_heads, carry, None, None)
            return n + 1, stick_left(carry[0], carry[1]), carry

        state = (jnp.int32(1), stick_left(carry[0], carry[1]), carry)
        state = lax.while_loop(lambda st, i=i: (st[0] < i) & (st[1] > 0), earlier, state)
        o_ref[0, sub * t:(sub + 1) * t, :] = state[2][2].T.astype(BF16)


def _token_attn(kernel, name, t, q, k, v, extra=(), scratch=()):
    b, s, width = q.shape
    q_spec = pl.BlockSpec((1, t, LANES), lambda bi, p, i: (bi, i, p))
    kv_spec = pl.BlockSpec((1, s, LANES), lambda bi, p, i: (bi, 0, p))
    extra_specs = [pl.BlockSpec(a.shape, lambda bi, p, i: (0, 0)) for a in extra]
    return pl.pallas_call(
        kernel,
        out_shape=jax.ShapeDtypeStruct((b, s, width), BF16),
        grid=(b, width // LANES, s // t),
        in_specs=[q_spec, kv_spec, kv_spec] + extra_specs,
        out_specs=q_spec,
        scratch_shapes=list(scratch),
        compiler_params=_params("parallel", "parallel", "arbitrary"),
        name=name,
    )(q, k, v, *extra)


def _diff_kernel(q_ref, k_ref, v_ref, lam_ref, g_ref, o_ref, *s_refs, out_scale):
    t = q_ref.shape[1]
    i = pl.program_id(2)
    q = q_ref[0]
    lane = lax.broadcasted_iota(jnp.int32, (1, LANES), 1)
    first = lane < HEAD_DIM
    q_maps = (jnp.where(first, q, 0), jnp.where(first, 0, q))
    key = lax.broadcasted_iota(jnp.int32, (t, t), 0)
    qry = lax.broadcasted_iota(jnp.int32, (t, t), 1)
    causal = key <= qry

    def scores(j, slot, mask):
        kb = k_ref[0, pl.ds(j * t, t), :]
        tops = []
        for c in range(2):
            s = _dot_nt(kb, q_maps[c])
            if mask is not None:
                s = jnp.where(mask, s, NEG_BIG)
            s_refs[slot][c] = s
            tops.append(jnp.max(s, axis=0, keepdims=True))
        return tuple(tops)

    def absorb(j, slot, tops, carry):
        vb = v_ref[0, pl.ds(j * t, t), :]
        new = []
        for c in range(2):
            m, l, acc = carry[c]
            m_new = jnp.maximum(m, tops[c])
            alpha = jnp.exp2(m - m_new)
            p = jnp.exp2(s_refs[slot][c] - m_new)
            l = alpha * l + jnp.sum(p, axis=0, keepdims=True)
            pv = lax.dot_general(vb, p.astype(BF16), (((0,), (0,)), ((), ())),
                                 preferred_element_type=F32)
            new.append((m_new, l, alpha * acc + pv))
        return tuple(new)

    def step(n, slot, state):
        tops, carry = state
        tops_next = scores(n, 1 - slot, None)
        return tops_next, absorb(jnp.where(n == 0, i, n - 1), slot, tops, carry)

    def pair(n2, state):
        return step(2 * n2 + 1, 1, step(2 * n2, 0, state))

    def tail_even(state):
        tops, carry = state
        return absorb(jnp.maximum(i - 1, 0), 0, tops, carry)

    def tail_odd(state):
        tops, carry = step(i - 1, 0, state)
        return absorb(i - 1, 1, tops, carry)

    init = (jnp.full((1, t), NEG_BIG, F32), jnp.zeros((1, t), F32), jnp.zeros((LANES, t), F32))
    state = lax.fori_loop(0, i // 2, pair, (scores(i, 0, causal), (init, init)))
    carry = lax.cond(i % 2 == 1, tail_odd, tail_even, state)
    (_, l1, acc1), (_, l2, acc2) = carry
    o = acc1 / l1 - lam_ref[...] * (acc2 / l2)
    o = o * lax.rsqrt(jnp.mean(o * o, axis=0, keepdims=True) + EPS)
    o_ref[0] = (o.T * (g_ref[...] * out_scale)).astype(BF16)


def _top2_route(logits):
    lane = lax.broadcasted_iota(jnp.int32, logits.shape, 1)
    lg = jnp.where(lane < N_EXPERTS, logits, -jnp.inf)
    m1 = jnp.max(lg, axis=-1, keepdims=True)
    i1 = jnp.min(jnp.where(lg == m1, lane, LANES), axis=-1, keepdims=True)
    lg2 = jnp.where(lane == i1, -jnp.inf, lg)
    m2 = jnp.max(lg2, axis=-1, keepdims=True)
    i2 = jnp.min(jnp.where(lg2 == m2, lane, LANES), axis=-1, keepdims=True)
    e = jnp.exp(m2 - m1)
    w1 = 1.0 / (1.0 + e)
    fields = (i1.astype(F32), i2.astype(F32), w1, e * w1)
    out = jnp.zeros(logits.shape, F32)
    for n, val in enumerate(fields):
        out = jnp.where(lane == n, val, out)
    return out


def _out_proj_kernel(*refs, moe):
    if moe:
        mix_ref, qm_ref, km_ref, vm_ref, w_ref, x_ref, g_ref, wr_ref, x_out, h_out, c_out = refs
    else:
        mix_ref, qm_ref, km_ref, vm_ref, w_ref, x_ref, g_ref, x_out, h_out = refs
    mix_width = mix_ref.shape[-1]
    o_mem = _mem_attn(qm_ref[...], km_ref[...], vm_ref[...])
    x = x_ref[...] + _dot(mix_ref[...], w_ref[:mix_width, :]) + _dot(o_mem, w_ref[mix_width:, :])
    x_out[...] = x
    h = _rms(x, g_ref[...])
    h_out[...] = h.astype(h_out.dtype)
    if moe:
        h_hi, h_lo = _split_bf16(h)
        w_hi, w_lo = _split_bf16(wr_ref[...])
        c_out[...] = _top2_route(_dot(h_hi, w_hi) + _dot(h_hi, w_lo) + _dot(h_lo, w_hi))


def _out_proj(o_mix, qm, km, vm, w, layer, x2d, g, w_router):
    rows, d_model = x2d.shape
    tm = ROW_TILE
    seq = rows // km.shape[1]
    row_spec = lambda n: pl.BlockSpec((tm, n), lambda i: (i, 0))
    const_spec = lambda a: pl.BlockSpec(a.shape, lambda i: (0, 0))
    mem_spec = pl.BlockSpec((None, None) + km.shape[2:], lambda i: (layer, i * tm // seq, 0, 0))
    moe = w_router is not None
    in_specs = [row_spec(o_mix.shape[1]), row_spec(qm.shape[1]), mem_spec, mem_spec,
                _layer_spec(w, layer), row_spec(d_model), const_spec(g)]
    args = [o_mix, qm, km, vm, w, x2d, g]
    out_shape = [jax.ShapeDtypeStruct((rows, d_model), F32),
                 jax.ShapeDtypeStruct((rows, d_model), F32 if moe else BF16)]
    out_specs = [row_spec(d_model), row_spec(d_model)]
    if moe:
        in_specs.append(const_spec(w_router))
        args.append(w_router)
        out_shape.append(jax.ShapeDtypeStruct((rows, LANES), F32))
        out_specs.append(row_spec(LANES))
    return pl.pallas_call(
        functools.partial(_out_proj_kernel, moe=moe),
        out_shape=tuple(out_shape),
        grid=(rows // tm,),
        in_specs=in_specs,
        out_specs=tuple(out_specs),
        compiler_params=_params("parallel"),
        name="out_proj_moe" if moe else "out_proj",
    )(*args)


def _swiglu_chunk(h, wg, wu, wd):
    g = _dot(h, wg)
    u = _dot(h, wu)
    return _dot((g * jax.nn.sigmoid(g) * u).astype(BF16), wd)


def _dense_ffn_kernel(h_ref, wgu_ref, wd_ref, x_ref, o_ref):
    d_ff = wd_ref.shape[0]
    o_ref[...] = x_ref[...] + _swiglu_chunk(h_ref[...], wgu_ref[:, :d_ff], wgu_ref[:, d_ff:],
                                            wd_ref[...])


def _dense_ffn(h, w_gate_up, w_down, layer, x2d):
    rows, d_model = x2d.shape
    tm = FFN_ROW_TILE
    resident = lambda a: _layer_spec(a, layer, pipeline_mode=pl.Buffered(1))
    row_spec = pl.BlockSpec((tm, d_model), lambda i: (i, 0))
    return pl.pallas_call(
        _dense_ffn_kernel,
        out_shape=jax.ShapeDtypeStruct((rows, d_model), F32),
        grid=(rows // tm,),
        in_specs=[row_spec, resident(w_gate_up), resident(w_down), row_spec],
        out_specs=row_spec,
        compiler_params=_params("parallel"),
        name="dense_ffn",
    )(h, w_gate_up, w_down, x2d)


def _route_plan(route, tm):
    n_tokens = route.shape[0]
    expert = jnp.concatenate([route[:, 0], route[:, 1]]).astype(jnp.int32)
    onehot = (expert[:, None] == jnp.arange(N_EXPERTS, dtype=jnp.int32)[None, :]).astype(jnp.int32)
    csum = jnp.cumsum(onehot, axis=0)
    tiles = (csum[-1] + tm - 1) // tm
    tile_end = jnp.cumsum(tiles)
    start = (tile_end - tiles) * tm
    pos = jnp.sum(onehot * (start[None, :] + csum - 1), axis=1)
    n_tiles = 2 * n_tokens // tm + N_EXPERTS
    tile_expert = jnp.sum(jnp.arange(n_tiles, dtype=jnp.int32)[:, None] >= tile_end[None, :], axis=1)
    n_used = tile_end[-1]
    tile_expert = jnp.minimum(tile_expert, tile_expert[n_used - 1])
    pos = pos.astype(jnp.int32)
    token = jnp.arange(2 * n_tokens, dtype=jnp.int32) % n_tokens
    src = jnp.zeros((n_tiles * tm,), jnp.int32).at[pos].set(token, unique_indices=True)
    return pos, src, jnp.concatenate([tile_expert, n_used[None]]).astype(jnp.int32)


def _for_each_row(n_rows, fn):
    def trip(g, c):
        for u in range(DMA_UNROLL):
            fn(g, u)
        return c

    lax.fori_loop(0, n_rows // DMA_UNROLL, trip, 0)


def _expert_ffn_kernel(plan_ref, src_ref, nxt_ref, h_ref, wg_ref, wu_ref, wd_ref, ys_ref,
                       hs_ref, hb_ref, acc_ref, sem, *, nf):
    i, f = pl.program_id(0), pl.program_id(1)
    n_tiles = pl.num_programs(0)
    n_used = plan_ref[n_tiles]
    used = i < n_used
    fetched = (i == 0) | (i - 1 < n_used)
    tm, d_model = hb_ref.shape
    share = tm // nf // DMA_UNROLL
    slot = i % 2

    def row_copy(idx_ref, g, u, s):
        token = idx_ref[0, 0, g * DMA_UNROLL + u]
        return pltpu.make_async_copy(h_ref.at[pl.ds(token, 1)],
                                     hs_ref.at[s, g, pl.ds(u, 1)], sem.at[s])

    @pl.when((i == 0) & (f == 0))
    def _():
        _for_each_row(tm, lambda g, u: row_copy(src_ref, g, u, 0).start())

    @pl.when(fetched & (f == 0))
    def _():
        _for_each_row(tm, lambda g, u: row_copy(src_ref, g, u, slot).wait())

    @pl.when(used & (f == 0))
    def _():
        hb_ref[...] = hs_ref[slot].reshape(tm, d_model).astype(BF16)
        acc_ref[...] = jnp.zeros_like(acc_ref)

    @pl.when(used)
    def _():
        for g in range(share):
            for u in range(DMA_UNROLL):
                row_copy(nxt_ref, f * share + g, u, 1 - slot).start()
        acc_ref[...] += _swiglu_chunk(hb_ref[...], wg_ref[0], wu_ref[0], wd_ref[0])

    @pl.when(used & (f == nf - 1))
    def _():
        ys_ref[...] = acc_ref[...]

    @pl.when(used & (i == n_tiles - 1) & (f == nf - 1))
    def _():
        _for_each_row(tm, lambda g, u: row_copy(nxt_ref, g, u, 1 - slot).wait())

    @pl.when(jnp.logical_not(used) & (f == nf - 1))
    def _():
        ys_ref[...] = jnp.zeros_like(ys_ref)


def _expert_ffn(h, src, plan, w_gate_up, w_down, layer, *, ff_tile):
    d_model = h.shape[1]
    d_ff = w_down.shape[2]
    nf = d_ff // ff_tile
    tm = MOE_ROW_TILE
    n_tiles = plan.shape[0] - 1
    n_rows = n_tiles * tm

    def chunk_of(i, f, plan_ref):
        return jnp.where(i < plan_ref[n_tiles], f, nf - 1)

    grid_spec = pltpu.PrefetchScalarGridSpec(
        num_scalar_prefetch=1,
        grid=(n_tiles, nf),
        in_specs=[
            pl.BlockSpec((1, 1, tm), lambda i, f, p: (i, 0, 0), memory_space=pltpu.SMEM),
            pl.BlockSpec((1, 1, tm), lambda i, f, p: (jnp.minimum(i + 1, n_tiles - 1), 0, 0),
                         memory_space=pltpu.SMEM),
            pl.BlockSpec(memory_space=pl.ANY),
            pl.BlockSpec((None, 1, d_model, ff_tile),
                         lambda i, f, p: (layer, p[i], 0, chunk_of(i, f, p))),
            pl.BlockSpec((None, 1, d_model, ff_tile),
                         lambda i, f, p: (layer, p[i], 0, chunk_of(i, f, p) + nf)),
            pl.BlockSpec((None, 1, ff_tile, d_model),
                         lambda i, f, p: (layer, p[i], chunk_of(i, f, p), 0)),
        ],
        out_specs=pl.BlockSpec((tm, d_model), lambda i, f, p: (i, 0)),
        scratch_shapes=[pltpu.VMEM((2, tm // DMA_UNROLL, DMA_UNROLL, d_model), F32),
                        pltpu.VMEM((tm, d_model), BF16),
                        pltpu.VMEM((tm, d_model), F32), pltpu.SemaphoreType.DMA((2,))],
    )
    src3 = src.reshape(n_tiles, 1, tm)
    return pl.pallas_call(
        functools.partial(_expert_ffn_kernel, nf=nf),
        out_shape=jax.ShapeDtypeStruct((n_rows, d_model), F32),
        grid_spec=grid_spec,
        compiler_params=_params("arbitrary", "arbitrary"),
        name="moe_experts",
    )(plan, src3, src3, h, w_gate_up, w_gate_up, w_down)


def _combine_kernel(p1_ref, p2_ref, ys_ref, route_ref, x_ref, o_ref, buf_ref, sem):
    chunk, d_model = x_ref.shape

    def row_copy(k, pos_ref, g, u):
        row = pos_ref[0, 0, g * DMA_UNROLL + u]
        return pltpu.make_async_copy(ys_ref.at[pl.ds(row, 1)],
                                     buf_ref.at[k, g, pl.ds(u, 1)], sem.at[k])

    def start(g, u):
        row_copy(0, p1_ref, g, u).start(priority=0)
        row_copy(1, p2_ref, g, u).start(priority=1)

    def wait(g, u):
        row_copy(0, p1_ref, g, u).wait()
        row_copy(1, p2_ref, g, u).wait()

    _for_each_row(chunk, start)
    _for_each_row(chunk, wait)
    route = route_ref[...]
    y1 = buf_ref[0].reshape(chunk, d_model)
    y2 = buf_ref[1].reshape(chunk, d_model)
    o_ref[...] = x_ref[...] + route[:, 2:3] * y1 + route[:, 3:4] * y2


def _combine(ys, pos, route, x2d):
    rows, d_model = x2d.shape
    chunk = ROUTE_CHUNK
    n_chunks = rows // chunk
    pos3 = pos.reshape(2 * n_chunks, 1, chunk)
    smem_spec = lambda off: pl.BlockSpec((1, 1, chunk), lambda c: (c + off, 0, 0), memory_space=pltpu.SMEM)
    row_spec = lambda n: pl.BlockSpec((chunk, n), lambda c: (c, 0))
    return pl.pallas_call(
        _combine_kernel,
        out_shape=jax.ShapeDtypeStruct((rows, d_model), F32),
        grid=(n_chunks,),
        in_specs=[smem_spec(0), smem_spec(n_chunks), pl.BlockSpec(memory_space=pl.ANY),
                  row_spec(LANES), row_spec(d_model)],
        out_specs=row_spec(d_model),
        scratch_shapes=[pltpu.VMEM((2, chunk // DMA_UNROLL, DMA_UNROLL, d_model), F32),
                        pltpu.SemaphoreType.DMA((2,))],
        compiler_params=_params("arbitrary"),
        name="moe_combine",
    )(pos3, pos3, ys, route, x2d)


def _moe_ffn(h, route, w_gate_up, w_down, layer, x2d, *, ff_tile):
    pos, src, plan = _route_plan(route, MOE_ROW_TILE)
    ys = _expert_ffn(h, src, plan, w_gate_up, w_down, layer, ff_tile=ff_tile)
    return _combine(ys, pos, route, x2d)


def _rope_tables(positions):
    half = ROPE_DIM // 2
    inv_freq = ROPE_THETA ** (-jnp.arange(0, ROPE_DIM, 2, dtype=F32) / ROPE_DIM)
    ang = positions.astype(F32).reshape(-1, 1) * inv_freq
    cos_sin = jnp.concatenate([jnp.cos(ang), jnp.sin(ang)], axis=-1)
    spread = np.zeros((ROPE_DIM, 3 * LANES), np.float32)
    offset = np.zeros((1, 3 * LANES), np.float32)
    for lane in range(LANES):
        m = lane % HEAD_DIM
        if m < ROPE_DIM:
            spread[m % half, lane] = 1.0
        else:
            offset[0, lane] = 1.0
        if m < half:
            spread[half + m, LANES + lane] = -1.0
        elif m < ROPE_DIM:
            spread[half + m - half, 2 * LANES + lane] = 1.0
    return cos_sin, jnp.asarray(spread), jnp.asarray(offset)


def _tile_gain(g, width):
    return jnp.tile(g.astype(F32), width // g.shape[-1]).reshape(1, width)


def _ff_tile(d_ff, limit):
    for step in (MXU_WIDTH, LANES):
        fits = [t for t in range(step, limit + 1, step) if d_ff % t == 0 and t > step]
        if fits:
            return max(fits)
    return LANES


def kernel(x, mem, positions, attn_norm, w_in, w_out, mem_norm, w_mem_kv, mem_q_norm, mem_k_norm,
           diff_q_norm, diff_k_norm, diff_lambda, diff_subln, ffn_norm, dense_w_gate_up,
           dense_w_down, w_router, moe_w_gate_up, moe_w_down):
    b, s, d_model = x.shape
    depth = w_in.shape[0]
    mem_len = mem.shape[1]
    mem_width = w_mem_kv.shape[-1] // 2
    mix_width = (w_in.shape[-1] - mem_width) // 3
    rows = b * s
    assert (s % max(ROW_TILE, FFN_ROW_TILE, SB_GROUP * SB_TILE, DIFF_TILE) == 0
            and mix_width % (2 * LANES) == 0)

    row = lambda a: a.astype(F32).reshape(1, -1)
    k_gain = jnp.stack([_tile_gain(mem_k_norm[i], mem_width) for i in range(depth)])
    km, vm = _mem_kv(mem.reshape(b * mem_len, d_model), row(mem_norm), w_mem_kv.astype(BF16), k_gain)
    km = km.reshape(depth, b, mem_len, mem_width)
    vm = vm.reshape(depth, b, mem_len, mem_width)
    rope_tables = _rope_tables(positions)
    w_in, w_out, dense_w_gate_up, dense_w_down, moe_w_gate_up, moe_w_down = (
        w.astype(BF16) for w in (w_in, w_out, dense_w_gate_up, dense_w_down, moe_w_gate_up, moe_w_down))

    x2d = x.reshape(rows, d_model)
    for i in range(depth):
        j = i // 2
        is_diff = i % 2 == 1
        diff_args = None
        if is_diff:
            diff_args = (_tile_gain(diff_q_norm[j], LANES), _tile_gain(diff_k_norm[j], LANES),
                         *rope_tables)
        q, k, v, qm = _in_proj(x2d, row(attn_norm[i]), w_in, i,
                               _tile_gain(mem_q_norm[i], LANES), diff_args, mix_width=mix_width)
        to3 = lambda a: a.reshape(b, s, a.shape[-1])
        if is_diff:
            lam_init = 0.8 - 0.6 * math.exp(-0.3 * i)
            lp = diff_lambda[j].astype(F32)
            lam = jnp.exp(jnp.sum(lp[0] * lp[1])) - jnp.exp(jnp.sum(lp[2] * lp[3])) + lam_init
            o_mix = _token_attn(functools.partial(_diff_kernel, out_scale=1.0 - lam_init), "diff_attn",
                                DIFF_TILE, to3(q), to3(k), to3(v),
                                extra=(lam.reshape(1, 1), row(diff_subln[j])),
                                scratch=(pltpu.VMEM((2, DIFF_TILE, DIFF_TILE), F32),) * 2)
        else:
            o_mix = _token_attn(_sb_kernel, "sb_attn", SB_GROUP * SB_TILE, to3(q), to3(k), to3(v))
        router = None
        if is_diff:
            router = jnp.pad(w_router[j].astype(F32), ((0, 0), (0, LANES - w_router.shape[-1])))
        outs = _out_proj(o_mix.reshape(rows, mix_width), qm, km, vm, w_out, i, x2d,
                         row(ffn_norm[i]), router)
        if is_diff:
            x2d, h, route = outs
            x2d = _moe_ffn(h, route, moe_w_gate_up, moe_w_down, j, x2d,
                           ff_tile=_ff_tile(moe_w_down.shape[-2], 2048))
        else:
            x2d, h = outs
            x2d = _dense_ffn(h, dense_w_gate_up, dense_w_down, j, x2d)
    return x2d.reshape(b, s, d_model)
```
